```python
import math
import jax, jax.numpy as jnp
from jax import lax
import numpy as np

D_MODEL = 1024
BATCH = 32
SEQ = 2048
DEPTH = 2
DEC_BATCH = 16
DEC_SEQ = 16
PAST_LEN = 4096

CHUNK = 64
Q_BLOCK = 128
HEAD_DIM = 64
N_HEADS_A = 8
N_HEADS_B = 8
WIDTH_A = N_HEADS_A * HEAD_DIM
WIDTH_B = N_HEADS_B * HEAD_DIM
MIX_WIDTH = WIDTH_A + WIDTH_B
IN_DIM = 4 * WIDTH_A + N_HEADS_A + 4 * WIDTH_B
EPS = 1e-6
FORGET_BIAS_INIT = 3.0

kernel_name = "fox_stickbreak_hybrid_stream_step"


def rmsnorm(x, g):
    xf = x.astype(jnp.float32)
    y = xf * lax.rsqrt(jnp.mean(xf * xf, axis=-1, keepdims=True) + EPS)
    return (y * g.astype(jnp.float32)).astype(x.dtype)


def project(h, w_in, b_f):
    b, t, _ = h.shape
    u = jnp.einsum("btd,de->bte", h, w_in)
    sizes = (WIDTH_A, WIDTH_A, WIDTH_A, N_HEADS_A, WIDTH_A,
             WIDTH_B, WIDTH_B, WIDTH_B, WIDTH_B)
    idx, acc = [], 0
    for s in sizes[:-1]:
        acc += s
        idx.append(acc)
    qa, ka, va, fa, za, qb, kb, vb, zb = jnp.split(u, idx, axis=-1)
    hd = lambda a, n: a.reshape(b, t, n, HEAD_DIM)
    logf = jax.nn.log_sigmoid((fa + b_f).astype(jnp.float32))
    return (hd(qa, N_HEADS_A), hd(ka, N_HEADS_A), hd(va, N_HEADS_A), logf, za,
            hd(qb, N_HEADS_B), hd(kb, N_HEADS_B), hd(vb, N_HEADS_B), zb)


def fox_attend(q, k, v, cq, ck, q_pos, k_pos):
    s = jnp.einsum("bqhd,bkhd->bhqk", q, k, preferred_element_type=jnp.float32) * (HEAD_DIM ** -0.5)
    bias = jnp.swapaxes(cq, 1, 2)[:, :, :, None] - jnp.swapaxes(ck, 1, 2)[:, :, None, :]
    mask = k_pos[None, :] <= q_pos[:, None]
    p = jax.nn.softmax(jnp.where(mask, s + bias, -jnp.inf), axis=-1)
    return jnp.einsum("bhqk,bkhd->bqhd", p.astype(v.dtype), v)


def sb_attend(q, k, v, q_pos, k_pos):
    z = jnp.einsum("bqhd,bkhd->bhqk", q, k, preferred_element_type=jnp.float32) * (HEAD_DIM ** -0.5)
    mask = k_pos[None, :] < q_pos[:, None]
    log_1m = jnp.where(mask, jax.nn.log_sigmoid(-z), 0.0)
    suffix = lax.cumsum(log_1m, axis=3, reverse=True) - log_1m
    a = jnp.where(mask, jnp.exp(jax.nn.log_sigmoid(z) + suffix), 0.0)
    return jnp.einsum("bhqk,bkhd->bqhd", a.astype(v.dtype), v)


def merge(oa, za, ob, zb, gn_a, gn_b, w_out):
    b, t = oa.shape[0], oa.shape[1]
    ya = rmsnorm(oa.reshape(b, t, WIDTH_A), gn_a) * jax.nn.silu(za)
    yb = rmsnorm(ob.reshape(b, t, WIDTH_B), gn_b) * jax.nn.silu(zb)
    return jnp.einsum("bte,ed->btd", jnp.concatenate([ya, yb], axis=-1), w_out)


def prompt_attention(qa, ka, va, logf, qb, kb, vb):
    t = qa.shape[1]
    c = jnp.cumsum(logf, axis=1)
    pos = jnp.arange(t)
    outs_a, outs_b = [], []
    for i in range(t // Q_BLOCK):
        lo, hi = i * Q_BLOCK, (i + 1) * Q_BLOCK
        outs_a.append(fox_attend(qa[:, lo:hi], ka[:, :hi], va[:, :hi],
                                 c[:, lo:hi], c[:, :hi], pos[lo:hi], pos[:hi]))
        outs_b.append(sb_attend(qb[:, lo:hi], kb[:, :hi], vb[:, :hi], pos[lo:hi], pos[:hi]))
    return jnp.concatenate(outs_a, axis=1), jnp.concatenate(outs_b, axis=1)


def sample_attention(qa, ka, va, logf, qb, kb, vb, ck_a, cv_a, cf_a, ck_b, cv_b):
    p, t = ck_a.shape[1], qa.shape[1]
    ka_all = jnp.concatenate([ck_a.astype(ka.dtype), ka], axis=1)
    va_all = jnp.concatenate([cv_a.astype(va.dtype), va], axis=1)
    kb_all = jnp.concatenate([ck_b.astype(kb.dtype), kb], axis=1)
    vb_all = jnp.concatenate([cv_b.astype(vb.dtype), vb], axis=1)
    c_all = jnp.cumsum(jnp.concatenate([cf_a.astype(jnp.float32), logf], axis=1), axis=1)
    k_pos = jnp.arange(p + t)
    q_pos = p + jnp.arange(t)
    oa = fox_attend(qa, ka_all, va_all, c_all[:, p:], c_all, q_pos, k_pos)
    ob = sb_attend(qb, kb_all, vb_all, q_pos, k_pos)
    return oa, ob


def setup_inputs(seed: int = 0) -> dict:
    key = jax.random.key(seed)
    ks = jax.random.split(key, 16)
    nrm = lambda k, shape: jax.random.normal(k, shape, jnp.float32)
    cache_a = (DEPTH, DEC_BATCH, PAST_LEN, N_HEADS_A, HEAD_DIM)
    cache_b = (DEPTH, DEC_BATCH, PAST_LEN, N_HEADS_B, HEAD_DIM)
    return {
        "x_prompt": nrm(ks[0], (BATCH, SEQ, D_MODEL)),
        "x_sample": nrm(ks[1], (DEC_BATCH, DEC_SEQ, D_MODEL)),
        "cache_fox_k": nrm(ks[2], cache_a),
        "cache_fox_v": nrm(ks[3], cache_a),
        "cache_fox_logf": jax.nn.log_sigmoid(FORGET_BIAS_INIT + nrm(ks[4], (DEPTH, DEC_BATCH, PAST_LEN, N_HEADS_A))),
        "cache_sb_k": nrm(ks[5], cache_b),
        "cache_sb_v": nrm(ks[6], cache_b),
        "norm_g": 1.0 + 0.01 * nrm(ks[7], (DEPTH, D_MODEL)),
        "w_in": nrm(ks[8], (DEPTH, D_MODEL, IN_DIM)) * D_MODEL ** -0.5,
        "b_f": FORGET_BIAS_INIT + 0.1 * nrm(ks[9], (DEPTH, N_HEADS_A)),
        "out_norm_a": 1.0 + 0.01 * nrm(ks[10], (DEPTH, WIDTH_A)),
        "out_norm_b": 1.0 + 0.01 * nrm(ks[11], (DEPTH, WIDTH_B)),
        "w_out": nrm(ks[12], (DEPTH, MIX_WIDTH, D_MODEL)) * MIX_WIDTH ** -0.5,
        "final_norm_g": 1.0 + 0.01 * nrm(ks[13], (D_MODEL,)),
    }


def reference(x_prompt, x_sample, cache_fox_k, cache_fox_v, cache_fox_logf, cache_sb_k, cache_sb_v,
              norm_g, w_in, b_f, out_norm_a, out_norm_b, w_out, final_norm_g):
    xp, xs = x_prompt, x_sample
    p_ka, p_va, p_fa, p_kb, p_vb = [], [], [], [], []
    s_ka, s_va, s_fa, s_kb, s_vb = [], [], [], [], []
    for l in range(DEPTH):
        hp = rmsnorm(xp, norm_g[l])
        qa, ka, va, logf, za, qb, kb, vb, zb = project(hp, w_in[l], b_f[l])
        oa, ob = prompt_attention(qa, ka, va, logf, qb, kb, vb)
        xp = xp + merge(oa, za, ob, zb, out_norm_a[l], out_norm_b[l], w_out[l])
        p_ka.append(ka); p_va.append(va); p_fa.append(logf); p_kb.append(kb); p_vb.append(vb)

        hs = rmsnorm(xs, norm_g[l])
        qa, ka, va, logf, za, qb, kb, vb, zb = project(hs, w_in[l], b_f[l])
        oa, ob = sample_attention(qa, ka, va, logf, qb, kb, vb,
                                  cache_fox_k[l], cache_fox_v[l], cache_fox_logf[l],
                                  cache_sb_k[l], cache_sb_v[l])
        xs = xs + merge(oa, za, ob, zb, out_norm_a[l], out_norm_b[l], w_out[l])
        s_ka.append(ka); s_va.append(va); s_fa.append(logf); s_kb.append(kb); s_vb.append(vb)

    y_prompt = rmsnorm(xp, final_norm_g)
    y_sample = rmsnorm(xs, final_norm_g)
    return (y_prompt, y_sample,
            jnp.stack(p_ka), jnp.stack(p_va), jnp.stack(p_fa), jnp.stack(p_kb), jnp.stack(p_vb),
            jnp.stack(s_ka), jnp.stack(s_va), jnp.stack(s_fa), jnp.stack(s_kb), jnp.stack(s_vb))
```

```python
import functools

import jax
import jax.numpy as jnp
from jax import lax
from jax.experimental import pallas as pl
from jax.experimental.pallas import tpu as pltpu

HEAD_DIM = 64
N_HEADS = 8
WIDTH = N_HEADS * HEAD_DIM
N_FORGET = N_HEADS
EPS = 1e-6
SCALE = HEAD_DIM ** -0.5
NEG_BIG = -1e30

LANES = 128
PAIR = 2 * HEAD_DIM
KEY_BLOCK = 128
QUERY_BLOCK = 256
PROJ_ROWS = 256
SAMPLE_CHUNK = 512
VMEM_LIMIT = 48 * 1024 * 1024

_f32 = jnp.float32
_bf16 = jnp.bfloat16


def _softplus(z):
    return jnp.maximum(z, 0.0) + jnp.log(1.0 + jnp.exp(-jnp.abs(z)))


def _log_sigmoid(x):
    return jnp.minimum(x, 0.0) - jnp.log(1.0 + jnp.exp(-jnp.abs(x)))


def _rmsnorm(x, g):
    return x * lax.rsqrt(jnp.mean(x * x, axis=-1, keepdims=True) + EPS) * g


def _split3(x):
    a1 = x.astype(_bf16)
    r1 = x - a1.astype(_f32)
    a2 = r1.astype(_bf16)
    a3 = (r1 - a2.astype(_f32)).astype(_bf16)
    return a1, a2, a3


def _split2_rows(x):
    hi = lax.bitcast_convert_type(
        lax.bitcast_convert_type(x, jnp.uint32) & jnp.uint32(0xFFFF0000), _f32)
    return jnp.concatenate([hi.astype(_bf16), (x - hi).astype(_bf16)], axis=0)


def _split2_cols(x):
    hi = lax.bitcast_convert_type(
        lax.bitcast_convert_type(x, jnp.uint32) & jnp.uint32(0xFFFF0000), _f32)
    return jnp.concatenate([hi.astype(_bf16), (x - hi).astype(_bf16)], axis=1)


def _lane_cumsum(x, carry):
    n = x.shape[1]
    r = lax.broadcasted_iota(jnp.int32, (n, n), 0)
    c = lax.broadcasted_iota(jnp.int32, (n, n), 1)
    tri = jnp.where(r <= c, 1.0, 0.0).astype(_bf16)
    a1, a2, a3 = _split3(x)
    dot = functools.partial(jnp.dot, preferred_element_type=_f32)
    return carry + (dot(a1, tri) + dot(a2, tri) + dot(a3, tri))


def _inproj_kernel(x_ref, g_ref, w_ref, wf_ref, bf_ref,
                   ka_ref, va_ref, kb_ref, vb_ref, logf_ref, ct_ref, qt_ref, kbf_ref, vt_ref, z_ref,
                   carry_ref):
    tm = x_ref.shape[1]
    h = _rmsnorm(x_ref[0], g_ref[...]).astype(_bf16)

    def seg(i):
        return jnp.dot(h, w_ref[:, i * WIDTH:(i + 1) * WIDTH], preferred_element_type=_f32)

    for grp, (k_ref, v_ref) in enumerate(((ka_ref, va_ref), (kb_ref, vb_ref))):
        lo, hi = grp * WIDTH, (grp + 1) * WIDTH
        q = seg(4 * grp + 0) * SCALE
        qt_ref[0, lo:hi, :] = q.T.astype(_bf16)
        k = seg(4 * grp + 1)
        k_ref[0] = k
        kbf_ref[0, :, lo:hi] = k.astype(_bf16)
        v = seg(4 * grp + 2)
        v_ref[0] = v
        for s in range(tm // KEY_BLOCK):
            vt_ref[0, s, lo:hi, :] = v[s * KEY_BLOCK:(s + 1) * KEY_BLOCK, :].T.astype(_bf16)
        z_ref[0, :, lo:hi] = seg(4 * grp + 3)

    fa = jnp.dot(h, wf_ref[...], preferred_element_type=_f32) + bf_ref[...]
    logf = _log_sigmoid(fa)
    logf_ref[0] = logf[:, 0:N_FORGET]

    @pl.when(pl.program_id(1) == 0)
    def _():
        carry_ref[...] = jnp.zeros_like(carry_ref)

    c = _lane_cumsum(logf.T[0:16, :], carry_ref[:, 0:1])
    ct_ref[0] = c[0:N_FORGET, :]
    carry_ref[...] = jnp.broadcast_to(c[:, tm - 1:tm], carry_ref.shape)


def _inproj(x, g, w_main, w_f, b_f):
    b, t, d = x.shape
    tm = PROJ_ROWS
    nkb = t // KEY_BLOCK
    row = lambda bi, ti: (bi, ti, 0)
    const2 = lambda bi, ti: (0, 0)
    out_shape = (
        jax.ShapeDtypeStruct((b, t, WIDTH), _f32),
        jax.ShapeDtypeStruct((b, t, WIDTH), _f32),
        jax.ShapeDtypeStruct((b, t, WIDTH), _f32),
        jax.ShapeDtypeStruct((b, t, WIDTH), _f32),
        jax.ShapeDtypeStruct((b, t, N_FORGET), _f32),
        jax.ShapeDtypeStruct((b, N_FORGET, t), _f32),
        jax.ShapeDtypeStruct((b, 2 * WIDTH, t), _bf16),
        jax.ShapeDtypeStruct((b, t, 2 * WIDTH), _bf16),
        jax.ShapeDtypeStruct((b, nkb, 2 * WIDTH, KEY_BLOCK), _bf16),
        jax.ShapeDtypeStruct((b, t, 2 * WIDTH), _f32),
    )
    out_specs = (
        pl.BlockSpec((1, tm, WIDTH), row),
        pl.BlockSpec((1, tm, WIDTH), row),
        pl.BlockSpec((1, tm, WIDTH), row),
        pl.BlockSpec((1, tm, WIDTH), row),
        pl.BlockSpec((1, tm, N_FORGET), row),
        pl.BlockSpec((1, N_FORGET, tm), lambda bi, ti: (bi, 0, ti)),
        pl.BlockSpec((1, 2 * WIDTH, tm), lambda bi, ti: (bi, 0, ti)),
        pl.BlockSpec((1, tm, 2 * WIDTH), row),
        pl.BlockSpec((1, tm // KEY_BLOCK, 2 * WIDTH, KEY_BLOCK), lambda bi, ti: (bi, ti, 0, 0)),
        pl.BlockSpec((1, tm, 2 * WIDTH), row),
    )
    return pl.pallas_call(
        _inproj_kernel,
        out_shape=out_shape,
        grid=(b, t // tm),
        in_specs=[
            pl.BlockSpec((1, tm, d), row),
            pl.BlockSpec((1, d), const2),
            pl.BlockSpec(w_main.shape, const2),
            pl.BlockSpec(w_f.shape, const2),
            pl.BlockSpec((1, LANES), const2),
        ],
        out_specs=out_specs,
        scratch_shapes=[pltpu.VMEM((16, LANES), _f32)],
        compiler_params=pltpu.CompilerParams(
            dimension_semantics=("arbitrary", "arbitrary"), vmem_limit_bytes=VMEM_LIMIT),
        name="prompt_inproj",
    )(x, g, w_main, w_f, b_f)


def _head_queries(qt):
    row = lax.broadcasted_iota(jnp.int32, qt.shape, 0)
    zero = jnp.zeros_like(qt)
    return jnp.where(row < HEAD_DIM, qt, zero), jnp.where(row >= HEAD_DIM, qt, zero)


def _fox_kernel(qt_ref, k_ref, vt_ref, ct_ref, o_ref):
    j = pl.program_id(2)
    qh = _head_queries(qt_ref[0])
    cq = [jnp.concatenate([ct_ref[0, 0, hh, pl.ds(2 * j, 1), :],
                           ct_ref[0, 0, hh, pl.ds(2 * j + 1, 1), :]], axis=1) for hh in range(2)]
    kio = lax.broadcasted_iota(jnp.int32, (KEY_BLOCK, QUERY_BLOCK), 0)
    qio = lax.broadcasted_iota(jnp.int32, (KEY_BLOCK, QUERY_BLOCK), 1)

    def block(kb, carry, masked):
        kblk = k_ref[0, pl.ds(pl.multiple_of(kb * KEY_BLOCK, KEY_BLOCK), KEY_BLOCK), :]
        new = []
        for hh in range(2):
            m, l, acc = carry[hh]
            s = jnp.dot(kblk, qh[hh], preferred_element_type=_f32)
            ck_row = ct_ref[0, 0, hh, pl.ds(kb, 1), :]
            ck = jnp.broadcast_to(ck_row, (KEY_BLOCK, KEY_BLOCK)).T
            ck = jnp.concatenate([ck] * (QUERY_BLOCK // KEY_BLOCK), axis=1)
            logit = s + (cq[hh] - ck)
            if masked:
                logit = jnp.where(kb * KEY_BLOCK + kio <= j * QUERY_BLOCK + qio, logit, NEG_BIG)
            m_new = jnp.maximum(m, jnp.max(logit, axis=0, keepdims=True))
            alpha = jnp.exp(m - m_new)
            p = jnp.exp(logit - m_new)
            l = alpha * l + jnp.sum(p, axis=0, keepdims=True)
            vth = vt_ref[0, kb, hh * HEAD_DIM:(hh + 1) * HEAD_DIM, :]
            acc = alpha * acc + jnp.dot(vth, p.astype(_bf16), preferred_element_type=_f32)
            new.append((m_new, l, acc))
        return tuple(new)

    init = tuple((jnp.full((1, QUERY_BLOCK), NEG_BIG, _f32), jnp.zeros((1, QUERY_BLOCK), _f32),
                  jnp.zeros((HEAD_DIM, QUERY_BLOCK), _f32)) for _ in range(2))
    nfull = (QUERY_BLOCK // KEY_BLOCK) * j
    carry = lax.fori_loop(0, nfull, lambda kb, c: block(kb, c, False), init)
    for d in range(QUERY_BLOCK // KEY_BLOCK):
        carry = block(nfull + d, carry, True)
    o = jnp.concatenate([acc / l for (_, l, acc) in carry], axis=0)
    o_ref[0] = o.T


def _sb_kernel(qt_ref, k_ref, vt_ref, o_ref):
    j = pl.program_id(2)
    qh = _head_queries(qt_ref[0])
    kio = lax.broadcasted_iota(jnp.int32, (KEY_BLOCK, QUERY_BLOCK), 0)
    qio = lax.broadcasted_iota(jnp.int32, (KEY_BLOCK, QUERY_BLOCK), 1)
    r = lax.broadcasted_iota(jnp.int32, (KEY_BLOCK, 2 * KEY_BLOCK), 0)
    c = lax.broadcasted_iota(jnp.int32, (KEY_BLOCK, 2 * KEY_BLOCK), 1)
    c = jnp.where(c >= KEY_BLOCK, c - KEY_BLOCK, c)
    suffix = jnp.where(c > r, 1.0, 0.0).astype(_bf16)

    def block(kb, carry, masked):
        kblk = k_ref[0, pl.ds(pl.multiple_of(kb * KEY_BLOCK, KEY_BLOCK), KEY_BLOCK), :]
        new = []
        for hh in range(2):
            later, acc = carry[hh]
            z = jnp.dot(kblk, qh[hh], preferred_element_type=_f32)
            sp = _softplus(z)
            if masked:
                msk = kb * KEY_BLOCK + kio < j * QUERY_BLOCK + qio
                sp = jnp.where(msk, sp, 0.0)
            within = jnp.dot(suffix, _split2_rows(sp), preferred_element_type=_f32)
            a = jnp.exp(z - sp - within - later)
            if masked:
                a = jnp.where(msk, a, 0.0)
            vth = vt_ref[0, kb, hh * HEAD_DIM:(hh + 1) * HEAD_DIM, :]
            acc = acc + jnp.dot(vth, a.astype(_bf16), preferred_element_type=_f32)
            later = later + jnp.sum(sp, axis=0, keepdims=True)
            new.append((later, acc))
        return tuple(new)

    carry = tuple((jnp.zeros((1, QUERY_BLOCK), _f32), jnp.zeros((HEAD_DIM, QUERY_BLOCK), _f32))
                  for _ in range(2))
    nfull = (QUERY_BLOCK // KEY_BLOCK) * j
    for d in reversed(range(QUERY_BLOCK // KEY_BLOCK)):
        carry = block(nfull + d, carry, True)
    carry = lax.fori_loop(0, nfull, lambda i, cr: block(nfull - 1 - i, cr, False), carry)
    o = jnp.concatenate([acc for (_, acc) in carry], axis=0)
    o_ref[0] = o.T


def _prompt_attention(kernel_fn, name, qt, kbf, vt, ct, group):
    b, _, t = qt.shape
    nkb = t // KEY_BLOCK
    npair = WIDTH // PAIR
    off = group * npair
    in_specs = [
        pl.BlockSpec((1, PAIR, QUERY_BLOCK), lambda bi, p, j: (bi, off + p, j)),
        pl.BlockSpec((1, t, PAIR), lambda bi, p, j: (bi, 0, off + p)),
        pl.BlockSpec((1, nkb, PAIR, KEY_BLOCK), lambda bi, p, j: (bi, 0, off + p, 0)),
    ]
    args = [qt, kbf, vt]
    if ct is not None:
        in_specs.append(pl.BlockSpec((1, 1, 2, nkb, KEY_BLOCK), lambda bi, p, j: (bi, p, 0, 0, 0)))
        args.append(ct)
    return pl.pallas_call(
        kernel_fn,
        out_shape=jax.ShapeDtypeStruct((b, t, WIDTH), _f32),
        grid=(b, npair, t // QUERY_BLOCK),
        in_specs=in_specs,
        out_specs=pl.BlockSpec((1, QUERY_BLOCK, PAIR), lambda bi, p, j: (bi, j, p)),
        compiler_params=pltpu.CompilerParams(
            dimension_semantics=("arbitrary", "arbitrary", "arbitrary"), vmem_limit_bytes=VMEM_LIMIT),
        name=name,
    )(*args)


def _merge_kernel(oa_ref, ob_ref, z_ref, x_ref, ga_ref, gb_ref, w_ref, fg_ref, out_ref, *, final):
    def gated(o, g, z):
        return (_rmsnorm(o, g) * (z * (1.0 / (1.0 + jnp.exp(-z))))).astype(_bf16)

    z = z_ref[0]
    ya = gated(oa_ref[0], ga_ref[...], z[:, 0:WIDTH])
    yb = gated(ob_ref[0], gb_ref[...], z[:, WIDTH:2 * WIDTH])
    y = (jnp.dot(ya, w_ref[0:WIDTH, :], preferred_element_type=_f32)
         + jnp.dot(yb, w_ref[WIDTH:2 * WIDTH, :], preferred_element_type=_f32))
    xn = x_ref[0] + y
    out_ref[0] = _rmsnorm(xn, fg_ref[...]) if final else xn


def _merge(oa, ob, z, x, ga, gb, w_out, fg, final):
    b, t, d = x.shape
    tm = min(PROJ_ROWS, t)
    row = lambda bi, ti: (bi, ti, 0)
    const2 = lambda bi, ti: (0, 0)
    return pl.pallas_call(
        functools.partial(_merge_kernel, final=final),
        out_shape=jax.ShapeDtypeStruct((b, t, d), _f32),
        grid=(b, t // tm),
        in_specs=[
            pl.BlockSpec((1, tm, WIDTH), row),
            pl.BlockSpec((1, tm, WIDTH), row),
            pl.BlockSpec((1, tm, 2 * WIDTH), row),
            pl.BlockSpec((1, tm, d), row),
            pl.BlockSpec((1, WIDTH), const2),
            pl.BlockSpec((1, WIDTH), const2),
            pl.BlockSpec(w_out.shape, const2),
            pl.BlockSpec((1, d), const2),
        ],
        out_specs=pl.BlockSpec((1, tm, d), row),
        compiler_params=pltpu.CompilerParams(
            dimension_semantics=("arbitrary", "arbitrary"), vmem_limit_bytes=VMEM_LIMIT),
        name="merge_final" if final else "merge",
    )(oa, ob, z, x, ga, gb, w_out, fg)


def _sample_inproj_kernel(x_ref, g_ref, w_ref, wf_ref, bf_ref, u_ref, logf_ref, logft_ref):
    h = _rmsnorm(x_ref[...], g_ref[...]).astype(_bf16)
    u_ref[...] = jnp.dot(h, w_ref[...], preferred_element_type=_f32)
    fa = jnp.dot(h, wf_ref[...], preferred_element_type=_f32) + bf_ref[...]
    logf = _log_sigmoid(fa)
    logf_ref[...] = logf[:, 0:N_FORGET]
    logft_ref[...] = logf.T[0:N_FORGET, :]


def _sample_inproj(x2d, g, w_main, w_f, b_f):
    r, d = x2d.shape
    return pl.pallas_call(
        _sample_inproj_kernel,
        out_shape=(jax.ShapeDtypeStruct((r, w_main.shape[1]), _f32),
                   jax.ShapeDtypeStruct((r, N_FORGET), _f32),
                   jax.ShapeDtypeStruct((N_FORGET, r), _f32)),
        compiler_params=pltpu.CompilerParams(vmem_limit_bytes=VMEM_LIMIT),
        name="sample_inproj",
    )(x2d, g, w_main, w_f, b_f)


def _expand_heads(m, rows_per_head):
    return jnp.concatenate(
        [jnp.broadcast_to(m[h:h + 1, :], (rows_per_head, m.shape[1])) for h in range(N_HEADS)], axis=0)


def _block_diag_queries(q, tq):
    rows = N_HEADS * tq
    qq = jnp.concatenate([q] * N_HEADS, axis=0)
    r = lax.broadcasted_iota(jnp.int32, (rows, WIDTH), 0) // tq
    c = lax.broadcasted_iota(jnp.int32, (rows, WIDTH), 1) // HEAD_DIM
    return jnp.where(r == c, qq, 0.0).astype(_bf16)


def _pick_heads(o, tq):
    r = lax.broadcasted_iota(jnp.int32, o.shape, 0) // tq
    c = lax.broadcasted_iota(jnp.int32, o.shape, 1) // HEAD_DIM
    od = jnp.where(r == c, o, 0.0)
    out = od[0:tq, :]
    for h in range(1, N_HEADS):
        out = out + od[h * tq:(h + 1) * tq, :]
    return out


def _nt_dot(a, b):
    return lax.dot_general(a, b, (((1,), (1,)), ((), ())), preferred_element_type=_f32)


def _sample_attn_kernel(u_ref, lfn_ref, lfc_ref, ka_ref, va_ref, kb_ref, vb_ref, tri_ref,
                        oa_ref, ob_ref,
                        cc_ref, xnew_ref, m_ref, l_ref, acca_ref, later_ref, accb_ref, *, tq, nch):
    i = pl.program_id(1)
    kc = SAMPLE_CHUNK
    rows = N_HEADS * tq
    u = u_ref[0]
    qa = _block_diag_queries(u[:, 0:WIDTH] * SCALE, tq)
    qb = _block_diag_queries(u[:, 4 * WIDTH:5 * WIDTH] * SCALE, tq)
    rq = lax.broadcasted_iota(jnp.int32, (rows, LANES), 0) % tq
    ln = lax.broadcasted_iota(jnp.int32, (rows, LANES), 1)
    pad = jnp.zeros((LANES - tq, WIDTH), _f32)

    @pl.when(i == 0)
    def _():
        carry = jnp.zeros((16, 1), _f32)
        zrows = jnp.zeros((16 - N_FORGET, kc), _f32)
        for ch in range(nch):
            x = jnp.concatenate([lfc_ref[0, :, ch * kc:(ch + 1) * kc], zrows], axis=0)
            cch = _lane_cumsum(x, carry)
            cc_ref[ch] = cch
            carry = cch[:, kc - 1:kc]
        xn = jnp.concatenate([lfn_ref[0], jnp.zeros((16 - N_FORGET, LANES), _f32)], axis=0)
        cnew = _lane_cumsum(xn, carry)
        xnew = _expand_heads(cnew[0:N_HEADS, :], tq)
        xnew_ref[...] = xnew
        cq = jnp.sum(jnp.where(ln == rq, xnew, 0.0), axis=1, keepdims=True)

        knew = jnp.concatenate([u[:, WIDTH:2 * WIDTH], pad], axis=0).astype(_bf16)
        vnew = jnp.concatenate([u[:, 2 * WIDTH:3 * WIDTH], pad], axis=0).astype(_bf16)
        logit = jnp.where(ln <= rq, _nt_dot(qa, knew) + (cq - xnew), NEG_BIG)
        m = jnp.max(logit, axis=1, keepdims=True)
        p = jnp.exp(logit - m)
        m_ref[...] = m
        l_ref[...] = jnp.sum(p, axis=1, keepdims=True)
        acca_ref[...] = jnp.dot(p.astype(_bf16), vnew, preferred_element_type=_f32)

        knew = jnp.concatenate([u[:, 5 * WIDTH:6 * WIDTH], pad], axis=0).astype(_bf16)
        vnew = jnp.concatenate([u[:, 6 * WIDTH:7 * WIDTH], pad], axis=0).astype(_bf16)
        z = _nt_dot(qb, knew)
        msk = ln < rq
        sp = jnp.where(msk, _softplus(z), 0.0)
        r = lax.broadcasted_iota(jnp.int32, (2 * LANES, LANES), 0)
        c = lax.broadcasted_iota(jnp.int32, (2 * LANES, LANES), 1)
        r = jnp.where(r >= LANES, r - LANES, r)
        suffix = jnp.where(r > c, 1.0, 0.0).astype(_bf16)
        within = jnp.dot(_split2_cols(sp), suffix, preferred_element_type=_f32)
        a = jnp.where(msk, jnp.exp(z - sp - within), 0.0)
        accb_ref[...] = jnp.dot(a.astype(_bf16), vnew, preferred_element_type=_f32)
        later_ref[...] = jnp.sum(sp, axis=1, keepdims=True)

    xq = xnew_ref[...]
    cq = jnp.sum(jnp.where(ln == rq, xq, 0.0), axis=1, keepdims=True)
    ck = _expand_heads(cc_ref[i][0:N_HEADS, :], tq)
    logit = _nt_dot(qa, ka_ref[0, 0].astype(_bf16)) + (cq - ck)
    m_old = m_ref[...]
    m_new = jnp.maximum(m_old, jnp.max(logit, axis=1, keepdims=True))
    alpha = jnp.exp(m_old - m_new)
    p = jnp.exp(logit - m_new)
    m_ref[...] = m_new
    l_ref[...] = alpha * l_ref[...] + jnp.sum(p, axis=1, keepdims=True)
    acca_ref[...] = alpha * acca_ref[...] + jnp.dot(
        p.astype(_bf16), va_ref[0, 0].astype(_bf16), preferred_element_type=_f32)

    z = _nt_dot(qb, kb_ref[0, 0].astype(_bf16))
    sp = _softplus(z)
    within = jnp.dot(_split2_cols(sp), tri_ref[...], preferred_element_type=_f32)
    a = jnp.exp(z - sp - within - later_ref[...])
    accb_ref[...] = accb_ref[...] + jnp.dot(
        a.astype(_bf16), vb_ref[0, 0].astype(_bf16), preferred_element_type=_f32)
    later_ref[...] = later_ref[...] + jnp.sum(sp, axis=1, keepdims=True)

    @pl.when(i == nch - 1)
    def _():
        oa_ref[0] = _pick_heads(acca_ref[...] / l_ref[...], tq)
        ob_ref[0] = _pick_heads(accb_ref[...], tq)


def _sample_attention(layer, u, lf_new, lf_cache_t, ck_a, cv_a, ck_b, cv_b, tri):
    nb, tq, _ = u.shape
    past = ck_a.shape[2]
    kc = SAMPLE_CHUNK
    nch = past // kc
    rows = N_HEADS * tq
    fwd = lambda bi, i: (layer, bi, i, 0)
    rev = lambda bi, i: (layer, bi, nch - 1 - i, 0)
    per_b = lambda bi, i: (bi, 0, 0)
    return pl.pallas_call(
        functools.partial(_sample_attn_kernel, tq=tq, nch=nch),
        out_shape=(jax.ShapeDtypeStruct((nb, tq, WIDTH), _f32),
                   jax.ShapeDtypeStruct((nb, tq, WIDTH), _f32)),
        grid=(nb, nch),
        in_specs=[
            pl.BlockSpec((1, tq, u.shape[2]), per_b),
            pl.BlockSpec((1, N_FORGET, LANES), per_b),
            pl.BlockSpec((1, N_FORGET, past), per_b),
            pl.BlockSpec((1, 1, kc, WIDTH), fwd),
            pl.BlockSpec((1, 1, kc, WIDTH), fwd),
            pl.BlockSpec((1, 1, kc, WIDTH), rev),
            pl.BlockSpec((1, 1, kc, WIDTH), rev),
            pl.BlockSpec(tri.shape, lambda bi, i: (0, 0)),
        ],
        out_specs=(pl.BlockSpec((1, tq, WIDTH), per_b), pl.BlockSpec((1, tq, WIDTH), per_b)),
        scratch_shapes=[
            pltpu.VMEM((nch, 16, kc), _f32),
            pltpu.VMEM((rows, LANES), _f32),
            pltpu.VMEM((rows, 1), _f32),
            pltpu.VMEM((rows, 1), _f32),
            pltpu.VMEM((rows, WIDTH), _f32),
            pltpu.VMEM((rows, 1), _f32),
            pltpu.VMEM((rows, WIDTH), _f32),
        ],
        compiler_params=pltpu.CompilerParams(
            dimension_semantics=("arbitrary", "arbitrary"), vmem_limit_bytes=VMEM_LIMIT),
        name="sample_attention",
    )(u, lf_new, lf_cache_t, ck_a, cv_a, ck_b, cv_b, tri)


def _split_weights(w):
    f0 = 3 * WIDTH
    w_main = jnp.concatenate([w[:, 0:f0], w[:, f0 + N_FORGET:]], axis=1).astype(_bf16)
    w_f = jnp.pad(w[:, f0:f0 + N_FORGET], ((0, 0), (0, LANES - N_FORGET))).astype(_bf16)
    return w_main, w_f


def kernel(x_prompt, x_sample, cache_fox_k, cache_fox_v, cache_fox_logf, cache_sb_k, cache_sb_v,
           norm_g, w_in, b_f, out_norm_a, out_norm_b, w_out, final_norm_g):
    depth = norm_g.shape[0]
    b, t, d = x_prompt.shape
    nb, tq, _ = x_sample.shape
    past = cache_fox_k.shape[2]
    nkb = t // KEY_BLOCK
    assert t % QUERY_BLOCK == 0 and t % PROJ_ROWS == 0 and past % SAMPLE_CHUNK == 0
    assert tq <= 16 and (nb * tq) % 8 == 0

    kc = SAMPLE_CHUNK
    r = lax.broadcasted_iota(jnp.int32, (2 * kc, kc), 0) % kc
    c = lax.broadcasted_iota(jnp.int32, (2 * kc, kc), 1)
    tri = jnp.where(r > c, 1.0, 0.0).astype(_bf16)
    fg = final_norm_g.reshape(1, d)
    flat = lambda a: a.reshape(depth, nb, past, WIDTH)
    cache_fox_k, cache_fox_v = flat(cache_fox_k), flat(cache_fox_v)
    cache_sb_k, cache_sb_v = flat(cache_sb_k), flat(cache_sb_v)

    xp, xs = x_prompt, x_sample.reshape(1, nb * tq, d)
    prompt_new, sample_new = [], []
    for l in range(depth):
        w_main, w_f = _split_weights(w_in[l])
        g = norm_g[l].reshape(1, d)
        bf = jnp.pad(b_f[l], (0, LANES - N_FORGET)).reshape(1, LANES)
        ga, gb = out_norm_a[l].reshape(1, WIDTH), out_norm_b[l].reshape(1, WIDTH)
        wo = w_out[l].astype(_bf16)
        final = l == depth - 1

        ka, va, kb, vb, logf, ct, qt, kbf, vt, z = _inproj(xp, g, w_main, w_f, bf)
        ct = ct.reshape(b, WIDTH // PAIR, 2, nkb, KEY_BLOCK)
        oa = _prompt_attention(_fox_kernel, "fox_attention", qt, kbf, vt, ct, 0)
        ob = _prompt_attention(_sb_kernel, "sb_attention", qt, kbf, vt, None, 1)
        xp = _merge(oa, ob, z, xp, ga, gb, wo, fg, final)
        hd = lambda a: a.reshape(b, t, N_HEADS, HEAD_DIM)
        prompt_new.append((hd(ka), hd(va), logf, hd(kb), hd(vb)))

        u, logf_s, logft_s = _sample_inproj(xs[0], g, w_main, w_f, bf)
        u3 = u.reshape(nb, tq, u.shape[1])
        lf_new = jnp.pad(logft_s.reshape(N_FORGET, nb, tq).transpose(1, 0, 2),
                         ((0, 0), (0, 0), (0, LANES - tq)))
        lf_cache_t = cache_fox_logf[l].transpose(0, 2, 1)
        oa_s, ob_s = _sample_attention(l, u3, lf_new, lf_cache_t,
                                       cache_fox_k, cache_fox_v, cache_sb_k, cache_sb_v, tri)
        z_s = jnp.concatenate([u[:, 3 * WIDTH:4 * WIDTH], u[:, 7 * WIDTH:8 * WIDTH]], axis=1)
        xs = _merge(oa_s.reshape(1, nb * tq, WIDTH), ob_s.reshape(1, nb * tq, WIDTH),
                    z_s.reshape(1, nb * tq, 2 * WIDTH), xs, ga, gb, wo, fg, final)
        hs = lambda a: a.reshape(nb, tq, N_HEADS, HEAD_DIM)
        sample_new.append((hs(u[:, WIDTH:2 * WIDTH]), hs(u[:, 2 * WIDTH:3 * WIDTH]),
                           logf_s.reshape(nb, tq, N_FORGET),
                           hs(u[:, 5 * WIDTH:6 * WIDTH]), hs(u[:, 6 * WIDTH:7 * WIDTH])))

    stack = lambda items, i: jnp.stack([it[i] for it in items])
    return (xp, xs.reshape(nb, tq, d),
            *(stack(prompt_new, i) for i in range(5)),
            *(stack(sample_new, i) for i in range(5)))
```

```python
import functools

import jax
import jax.numpy as jnp
from jax import lax
from jax.experimental import pallas as pl
from jax.experimental.pallas import tpu as pltpu

HEAD_DIM = 64
N_HEADS = 8
WIDTH = N_HEADS * HEAD_DIM
N_FORGET = N_HEADS
EPS = 1e-6
SCALE = HEAD_DIM ** -0.5
LOG2E = 1.4426950408889634
NEG_BIG = -1e30

LANES = 128
KEY_BLOCK = 256
QUERY_BLOCK = 256
SLAB = 256
HEADS_PER_SLAB = SLAB // HEAD_DIM
HEADS_PER_STEP = 8
PROJ_ROWS = 256
SAMPLE_CHUNK = 512
VMEM_LIMIT = 48 * 1024 * 1024

_f32 = jnp.float32
_bf16 = jnp.bfloat16


def _softplus(z):
    neg_abs = lax.bitcast_convert_type(
        lax.bitcast_convert_type(z, jnp.uint32) | jnp.uint32(0x80000000), _f32)
    return jnp.maximum(z, 0.0) + jnp.log(1.0 + jnp.exp(neg_abs))


def _softplus2(zs):
    neg_abs = lax.bitcast_convert_type(
        lax.bitcast_convert_type(zs, jnp.uint32) | jnp.uint32(0x80000000), _f32)
    return jnp.maximum(zs, 0.0) + jnp.log(1.0 + jnp.exp2(neg_abs)) * LOG2E


def _log_sigmoid(x):
    return jnp.minimum(x, 0.0) - jnp.log(1.0 + jnp.exp(-jnp.abs(x)))


def _rmsnorm(x, g):
    return x * lax.rsqrt(jnp.mean(x * x, axis=-1, keepdims=True) + EPS) * g


def _split3(x):
    a1 = x.astype(_bf16)
    r1 = x - a1.astype(_f32)
    a2 = r1.astype(_bf16)
    a3 = (r1 - a2.astype(_f32)).astype(_bf16)
    return a1, a2, a3


def _split2_rows(x):
    hi = lax.bitcast_convert_type(
        lax.bitcast_convert_type(x, jnp.uint32) & jnp.uint32(0xFFFF0000), _f32)
    return jnp.concatenate([hi.astype(_bf16), (x - hi).astype(_bf16)], axis=0)


def _split2_cols(x):
    hi = lax.bitcast_convert_type(
        lax.bitcast_convert_type(x, jnp.uint32) & jnp.uint32(0xFFFF0000), _f32)
    return jnp.concatenate([hi.astype(_bf16), (x - hi).astype(_bf16)], axis=1)


def _lane_cumsum(x, carry):
    n = x.shape[1]
    r = lax.broadcasted_iota(jnp.int32, (n, n), 0)
    c = lax.broadcasted_iota(jnp.int32, (n, n), 1)
    tri = jnp.where(r <= c, 1.0, 0.0).astype(_bf16)
    a1, a2, a3 = _split3(x)
    dot = functools.partial(jnp.dot, preferred_element_type=_f32)
    return carry + (dot(a1, tri) + dot(a2, tri) + dot(a3, tri))


def _inproj_kernel(x_ref, g_ref, w_ref, wf_ref, bf_ref,
                   ka_ref, va_ref, kb_ref, vb_ref, logf_ref, ct_ref, qt_ref, kbf_ref, vt_ref, z_ref,
                   carry_ref):
    tm = x_ref.shape[1]
    h = _rmsnorm(x_ref[0], g_ref[...]).astype(_bf16)

    def seg(i):
        return jnp.dot(h, w_ref[:, i * WIDTH:(i + 1) * WIDTH], preferred_element_type=_f32)

    for grp, (k_ref, v_ref) in enumerate(((ka_ref, va_ref), (kb_ref, vb_ref))):
        lo, hi = grp * WIDTH, (grp + 1) * WIDTH
        q = seg(4 * grp + 0) * (SCALE * LOG2E)
        qt_ref[0, lo:hi, :] = q.T.astype(_bf16)
        k = seg(4 * grp + 1)
        k_ref[0] = k
        kbf_ref[0, :, lo:hi] = k.astype(_bf16)
        v = seg(4 * grp + 2)
        v_ref[0] = v
        for s in range(tm // KEY_BLOCK):
            vt_ref[0, s, lo:hi, :] = v[s * KEY_BLOCK:(s + 1) * KEY_BLOCK, :].T.astype(_bf16)
        z_ref[0, :, lo:hi] = seg(4 * grp + 3)

    fa = jnp.dot(h, wf_ref[...], preferred_element_type=_f32) + bf_ref[...]
    logf = _log_sigmoid(fa)
    logf_ref[0] = logf[:, 0:N_FORGET]

    @pl.when(pl.program_id(1) == 0)
    def _():
        carry_ref[...] = jnp.zeros_like(carry_ref)

    c = _lane_cumsum(logf.T[0:16, :], carry_ref[:, 0:1])
    ct_ref[0] = c[0:N_FORGET, :] * LOG2E
    carry_ref[...] = jnp.broadcast_to(c[:, tm - 1:tm], carry_ref.shape)


def _inproj(x, g, w_main, w_f, b_f):
    b, t, d = x.shape
    tm = PROJ_ROWS
    nkb = t // KEY_BLOCK
    row = lambda bi, ti: (bi, ti, 0)
    const2 = lambda bi, ti: (0, 0)
    out_shape = (
        jax.ShapeDtypeStruct((b, t, WIDTH), _f32),
        jax.ShapeDtypeStruct((b, t, WIDTH), _f32),
        jax.ShapeDtypeStruct((b, t, WIDTH), _f32),
        jax.ShapeDtypeStruct((b, t, WIDTH), _f32),
        jax.ShapeDtypeStruct((b, t, N_FORGET), _f32),
        jax.ShapeDtypeStruct((b, N_FORGET, t), _f32),
        jax.ShapeDtypeStruct((b, 2 * WIDTH, t), _bf16),
        jax.ShapeDtypeStruct((b, t, 2 * WIDTH), _bf16),
        jax.ShapeDtypeStruct((b, nkb, 2 * WIDTH, KEY_BLOCK), _bf16),
        jax.ShapeDtypeStruct((b, t, 2 * WIDTH), _f32),
    )
    out_specs = (
        pl.BlockSpec((1, tm, WIDTH), row),
        pl.BlockSpec((1, tm, WIDTH), row),
        pl.BlockSpec((1, tm, WIDTH), row),
        pl.BlockSpec((1, tm, WIDTH), row),
        pl.BlockSpec((1, tm, N_FORGET), row),
        pl.BlockSpec((1, N_FORGET, tm), lambda bi, ti: (bi, 0, ti)),
        pl.BlockSpec((1, 2 * WIDTH, tm), lambda bi, ti: (bi, 0, ti)),
        pl.BlockSpec((1, tm, 2 * WIDTH), row),
        pl.BlockSpec((1, tm // KEY_BLOCK, 2 * WIDTH, KEY_BLOCK), lambda bi, ti: (bi, ti, 0, 0)),
        pl.BlockSpec((1, tm, 2 * WIDTH), row),
    )
    return pl.pallas_call(
        _inproj_kernel,
        out_shape=out_shape,
        grid=(b, t // tm),
        in_specs=[
            pl.BlockSpec((1, tm, d), row),
            pl.BlockSpec((1, d), const2),
            pl.BlockSpec(w_main.shape, const2),
            pl.BlockSpec(w_f.shape, const2),
            pl.BlockSpec((1, LANES), const2),
        ],
        out_specs=out_specs,
        scratch_shapes=[pltpu.VMEM((16, LANES), _f32)],
        compiler_params=pltpu.CompilerParams(
            dimension_semantics=("arbitrary", "arbitrary"), vmem_limit_bytes=VMEM_LIMIT),
        name="prompt_inproj",
    )(x, g, w_main, w_f, b_f)


def _head_queries(qt_ref):
    out = []
    for s in range(qt_ref.shape[1] // SLAB):
        qt = qt_ref[0, s * SLAB:(s + 1) * SLAB, :]
        head = lax.broadcasted_iota(jnp.int32, qt.shape, 0) // HEAD_DIM
        out += [jnp.where(head == hh, qt, jnp.zeros_like(qt)) for hh in range(SLAB // HEAD_DIM)]
    return out


_FOX_STAGES = (("scores", (), ("s",)), ("weights", ("s",), ("p", "alpha")), ("values", ("p", "alpha"), ()))
_SB_STAGES = (("scores", (), ("z",)), ("survival", ("z",), ("sp",)), ("suffix_sums", ("sp",), ("w",)),
              ("weights", ("z", "w"), ("a",)), ("values", ("a",), ()))
_FOX_GAP = 4
_SB_GAP = 1


def _intermediate_type(name):
    tile = (KEY_BLOCK, QUERY_BLOCK)
    return {"s": (tile, _f32), "z": (tile, _f32), "p": (tile, _bf16), "sp": (tile, _bf16), "a": (tile, _bf16),
            "alpha": ((1, QUERY_BLOCK), _f32), "w": ((KEY_BLOCK + 16, QUERY_BLOCK), _f32)}[name]


def _handoff_keys(stages, nh, gap):
    keys = []
    for h in range(nh):
        done = min(len(stages), (nh - 1 - h) // gap + 1)
        made = {n for (_, _, w) in stages[:done] for n in w}
        need = {n for (_, r, _) in stages[done:] for n in r}
        keys += [(n, h) for n in sorted(made & need)]
    return keys


class _Block:
    def __init__(self, kb, mask, handoff):
        self.kb, self.mask, self.handoff, self.vals = kb, mask, handoff, {}

    def put(self, name, h, value):
        self.vals[(name, h)] = value
        if (name, h) in self.handoff:
            self.handoff[(name, h)][...] = value

    def get(self, name, h):
        if (name, h) in self.vals:
            return self.vals[(name, h)]
        return self.handoff[(name, h)][...]


class _HeadPipeline:
    def __init__(self, stages, fns, nh, gap):
        self.fns, self.nh, self.gap = [fns[name] for (name, _, _) in stages], nh, gap
        self.tail = gap * (len(stages) - 1)
        assert self.tail <= nh

    def _step(self, blk, step):
        for k, fn in enumerate(self.fns):
            if 0 <= step - self.gap * k < self.nh:
                fn(blk, step - self.gap * k)

    def advance(self, prev, cur):
        for step in range(self.nh):
            if prev is not None and step < self.tail:
                self._step(prev, self.nh + step)
            self._step(cur, step)

    def finish(self, prev):
        for step in range(self.tail):
            self._step(prev, self.nh + step)


def _key_slab(k_ref, kb, h):
    s = h // HEADS_PER_SLAB
    return k_ref[0, pl.ds(pl.multiple_of(kb * KEY_BLOCK, KEY_BLOCK), KEY_BLOCK), s * SLAB:(s + 1) * SLAB]


def _fox_kernel(qt_ref, k_ref, vt_ref, ct_ref, o_ref, m_ref, acc_ref, *handoff_refs):
    j = pl.program_id(2)
    nh = qt_ref.shape[1] // HEAD_DIM
    handoff = dict(zip(_handoff_keys(_FOX_STAGES, nh, _FOX_GAP), handoff_refs))
    qh = _head_queries(qt_ref)
    ndiag = QUERY_BLOCK // KEY_BLOCK
    nfull = ndiag * j
    cq = [jnp.concatenate([ct_ref[0, h, pl.ds(nfull + d, 1), :] for d in range(ndiag)], axis=1)
          for h in range(nh)]
    kio = lax.broadcasted_iota(jnp.int32, (KEY_BLOCK, QUERY_BLOCK), 0)
    qio = lax.broadcasted_iota(jnp.int32, (KEY_BLOCK, QUERY_BLOCK), 1)
    causal = [d * KEY_BLOCK + kio <= qio for d in range(ndiag)]
    ones_rows = (lax.broadcasted_iota(jnp.int32, (16, KEY_BLOCK), 0) == 0).astype(_bf16)

    m_ref[...] = jnp.full(m_ref.shape, NEG_BIG, _f32)
    acc_ref[...] = jnp.zeros(acc_ref.shape, _f32)

    def scores(blk, h):
        s = jnp.dot(_key_slab(k_ref, blk.kb, h), qh[h], preferred_element_type=_f32)
        blk.put("s", h, s if blk.mask is None else jnp.where(blk.mask, s, NEG_BIG))

    def weights(blk, h):
        ck_row = ct_ref[0, h, pl.ds(blk.kb, 1), :]
        ck = jnp.broadcast_to(ck_row, (LANES, KEY_BLOCK)).T
        ck = jnp.concatenate([ck] * (QUERY_BLOCK // LANES), axis=1)
        logit = blk.get("s", h) + (cq[h] - ck)
        m_old = m_ref[h]
        m_new = jnp.maximum(m_old, jnp.max(logit, axis=0, keepdims=True))
        blk.put("alpha", h, jnp.exp2(m_old - m_new))
        blk.put("p", h, jnp.exp2(logit - m_new).astype(_bf16))
        m_ref[h] = m_new

    def values(blk, h):
        vth = jnp.concatenate([vt_ref[0, blk.kb, h * HEAD_DIM:(h + 1) * HEAD_DIM, :], ones_rows], axis=0)
        acc_ref[h] = (blk.get("alpha", h) * acc_ref[h]
                      + jnp.dot(vth, blk.get("p", h), preferred_element_type=_f32))

    pipe = _HeadPipeline(_FOX_STAGES, {"scores": scores, "weights": weights, "values": values}, nh, _FOX_GAP)
    prev = None
    for d in range(ndiag):
        cur = _Block(nfull + d, causal[d], handoff)
        pipe.advance(prev, cur)
        prev = cur

    last_diag = nfull + ndiag - 1

    @pl.loop(0, nfull)
    def _(i):
        in_flight = jnp.where(i == 0, last_diag, nfull - i)
        pipe.advance(_Block(in_flight, None, handoff), _Block(nfull - 1 - i, None, handoff))

    pipe.finish(_Block(jnp.where(nfull == 0, last_diag, 0), None, handoff))
    o = jnp.concatenate([acc_ref[h, 0:HEAD_DIM] / acc_ref[h, HEAD_DIM:HEAD_DIM + 1] for h in range(nh)], axis=0)
    o_ref[0] = o.T


def _sb_kernel(qt_ref, k_ref, vt_ref, o_ref, later_ref, acc_ref, *handoff_refs):
    j = pl.program_id(2)
    nh = qt_ref.shape[1] // HEAD_DIM
    handoff = dict(zip(_handoff_keys(_SB_STAGES, nh, _SB_GAP), handoff_refs))
    qh = _head_queries(qt_ref)
    ndiag = QUERY_BLOCK // KEY_BLOCK
    nfull = ndiag * j
    kio = lax.broadcasted_iota(jnp.int32, (KEY_BLOCK, QUERY_BLOCK), 0)
    qio = lax.broadcasted_iota(jnp.int32, (KEY_BLOCK, QUERY_BLOCK), 1)
    strict = [d * KEY_BLOCK + kio < qio for d in range(ndiag)]
    r = lax.broadcasted_iota(jnp.int32, (KEY_BLOCK + 16, KEY_BLOCK), 0)
    c = lax.broadcasted_iota(jnp.int32, (KEY_BLOCK + 16, KEY_BLOCK), 1)
    suffix = jnp.where((c >= r) | (r == KEY_BLOCK), 1.0, 0.0).astype(_bf16)

    later_ref[...] = jnp.zeros(later_ref.shape, _f32)
    acc_ref[...] = jnp.zeros(acc_ref.shape, _f32)

    def scores(blk, h):
        z = jnp.dot(_key_slab(k_ref, blk.kb, h), qh[h], preferred_element_type=_f32)
        blk.put("z", h, z if blk.mask is None else jnp.where(blk.mask, z, NEG_BIG))

    def survival(blk, h):
        blk.put("sp", h, _softplus2(blk.get("z", h)).astype(_bf16))

    def suffix_sums(blk, h):
        blk.put("w", h, jnp.dot(suffix, blk.get("sp", h), preferred_element_type=_f32))

    def weights(blk, h):
        w = blk.get("w", h)
        later = later_ref[h]
        blk.put("a", h, jnp.exp2(blk.get("z", h) - w[0:KEY_BLOCK] - later).astype(_bf16))
        later_ref[h] = later + w[KEY_BLOCK:KEY_BLOCK + 1]

    def values(blk, h):
        vth = vt_ref[0, blk.kb, h * HEAD_DIM:(h + 1) * HEAD_DIM, :]
        acc_ref[h] = acc_ref[h] + jnp.dot(vth, blk.get("a", h), preferred_element_type=_f32)

    pipe = _HeadPipeline(_SB_STAGES, {"scores": scores, "survival": survival, "suffix_sums": suffix_sums,
                                      "weights": weights, "values": values}, nh, _SB_GAP)
    prev = None
    for d in reversed(range(ndiag)):
        cur = _Block(nfull + d, strict[d], handoff)
        pipe.advance(prev, cur)
        prev = cur

    @pl.loop(0, nfull)
    def _(i):
        pipe.advance(_Block(nfull - i, None, handoff), _Block(nfull - 1 - i, None, handoff))

    pipe.finish(_Block(0, None, handoff))
    o_ref[0] = jnp.concatenate([acc_ref[h] for h in range(nh)], axis=0).T


def _prompt_attention(kernel_fn, name, qt, kbf, vt, ct, group):
    b, _, t = qt.shape
    nkb = t // KEY_BLOCK
    wstep = HEADS_PER_STEP * HEAD_DIM
    nstep = WIDTH // wstep
    off = group * nstep
    in_specs = [
        pl.BlockSpec((1, wstep, QUERY_BLOCK), lambda bi, p, j: (bi, off + p, j)),
        pl.BlockSpec((1, t, wstep), lambda bi, p, j: (bi, 0, off + p)),
        pl.BlockSpec((1, nkb, wstep, KEY_BLOCK), lambda bi, p, j: (bi, 0, off + p, 0)),
    ]
    args = [qt, kbf, vt]
    acc_rows = HEAD_DIM
    if ct is not None:
        in_specs.append(pl.BlockSpec((1, HEADS_PER_STEP, nkb, KEY_BLOCK), lambda bi, p, j: (bi, p, 0, 0)))
        args.append(ct)
        acc_rows += 16
    stages, gap = (_FOX_STAGES, _FOX_GAP) if group == 0 else (_SB_STAGES, _SB_GAP)
    scratch = [pltpu.VMEM((HEADS_PER_STEP, 1, QUERY_BLOCK), _f32),
               pltpu.VMEM((HEADS_PER_STEP, acc_rows, QUERY_BLOCK), _f32)]
    scratch += [pltpu.VMEM(*_intermediate_type(name))
                for (name, _) in _handoff_keys(stages, HEADS_PER_STEP, gap)]
    return pl.pallas_call(
        kernel_fn,
        out_shape=jax.ShapeDtypeStruct((b, t, WIDTH), _f32),
        grid=(b, nstep, t // QUERY_BLOCK),
        in_specs=in_specs,
        out_specs=pl.BlockSpec((1, QUERY_BLOCK, wstep), lambda bi, p, j: (bi, j, p)),
        scratch_shapes=scratch,
        compiler_params=pltpu.CompilerParams(
            dimension_semantics=("arbitrary", "arbitrary", "arbitrary"), vmem_limit_bytes=VMEM_LIMIT),
        name=name,
    )(*args)


def _merge_kernel(oa_ref, ob_ref, z_ref, x_ref, ga_ref, gb_ref, w_ref, fg_ref, out_ref, *, final):
    def gated(o, g, z):
        return (_rmsnorm(o, g) * (z * (1.0 / (1.0 + jnp.exp(-z))))).astype(_bf16)

    z = z_ref[0]
    ya = gated(oa_ref[0], ga_ref[...], z[:, 0:WIDTH])
    yb = gated(ob_ref[0], gb_ref[...], z[:, WIDTH:2 * WIDTH])
    y = (jnp.dot(ya, w_ref[0:WIDTH, :], preferred_element_type=_f32)
         + jnp.dot(yb, w_ref[WIDTH:2 * WIDTH, :], preferred_element_type=_f32))
    xn = x_ref[0] + y
    out_ref[0] = _rmsnorm(xn, fg_ref[...]) if final else xn


def _merge(oa, ob, z, x, ga, gb, w_out, fg, final):
    b, t, d = x.shape
    tm = min(PROJ_ROWS, t)
    row = lambda bi, ti: (bi, ti, 0)
    const2 = lambda bi, ti: (0, 0)
    return pl.pallas_call(
        functools.partial(_merge_kernel, final=final),
        out_shape=jax.ShapeDtypeStruct((b, t, d), _f32),
        grid=(b, t // tm),
        in_specs=[
            pl.BlockSpec((1, tm, WIDTH), row),
            pl.BlockSpec((1, tm, WIDTH), row),
            pl.BlockSpec((1, tm, 2 * WIDTH), row),
            pl.BlockSpec((1, tm, d), row),
            pl.BlockSpec((1, WIDTH), const2),
            pl.BlockSpec((1, WIDTH), const2),
            pl.BlockSpec(w_out.shape, const2),
            pl.BlockSpec((1, d), const2),
        ],
        out_specs=pl.BlockSpec((1, tm, d), row),
        compiler_params=pltpu.CompilerParams(
            dimension_semantics=("arbitrary", "arbitrary"), vmem_limit_bytes=VMEM_LIMIT),
        name="merge_final" if final else "merge",
    )(oa, ob, z, x, ga, gb, w_out, fg)


def _sample_inproj_kernel(x_ref, g_ref, w_ref, wf_ref, bf_ref, u_ref, logf_ref, logft_ref):
    h = _rmsnorm(x_ref[...], g_ref[...]).astype(_bf16)
    u_ref[...] = jnp.dot(h, w_ref[...], preferred_element_type=_f32)
    fa = jnp.dot(h, wf_ref[...], preferred_element_type=_f32) + bf_ref[...]
    logf = _log_sigmoid(fa)
    logf_ref[...] = logf[:, 0:N_FORGET]
    logft_ref[...] = logf.T[0:N_FORGET, :]


def _sample_inproj(x2d, g, w_main, w_f, b_f):
    r, d = x2d.shape
    return pl.pallas_call(
        _sample_inproj_kernel,
        out_shape=(jax.ShapeDtypeStruct((r, w_main.shape[1]), _f32),
                   jax.ShapeDtypeStruct((r, N_FORGET), _f32),
                   jax.ShapeDtypeStruct((N_FORGET, r), _f32)),
        compiler_params=pltpu.CompilerParams(vmem_limit_bytes=VMEM_LIMIT),
        name="sample_inproj",
    )(x2d, g, w_main, w_f, b_f)


def _expand_heads(m, rows_per_head):
    return jnp.concatenate(
        [jnp.broadcast_to(m[h:h + 1, :], (rows_per_head, m.shape[1])) for h in range(N_HEADS)], axis=0)


def _block_diag_queries(q, tq):
    rows = N_HEADS * tq
    qq = jnp.concatenate([q] * N_HEADS, axis=0)
    r = lax.broadcasted_iota(jnp.int32, (rows, WIDTH), 0) // tq
    c = lax.broadcasted_iota(jnp.int32, (rows, WIDTH), 1) // HEAD_DIM
    return jnp.where(r == c, qq, 0.0).astype(_bf16)


def _pick_heads(o, tq):
    r = lax.broadcasted_iota(jnp.int32, o.shape, 0) // tq
    c = lax.broadcasted_iota(jnp.int32, o.shape, 1) // HEAD_DIM
    od = jnp.where(r == c, o, 0.0)
    out = od[0:tq, :]
    for h in range(1, N_HEADS):
        out = out + od[h * tq:(h + 1) * tq, :]
    return out


def _nt_dot(a, b):
    return lax.dot_general(a, b, (((1,), (1,)), ((), ())), preferred_element_type=_f32)


def _sample_attn_kernel(u_ref, lfn_ref, lfc_ref, ka_ref, va_ref, kb_ref, vb_ref, tri_ref,
                        oa_ref, ob_ref,
                        cc_ref, xnew_ref, m_ref, l_ref, acca_ref, later_ref, accb_ref, *, tq, nch):
    i = pl.program_id(1)
    kc = SAMPLE_CHUNK
    rows = N_HEADS * tq
    u = u_ref[0]
    qa = _block_diag_queries(u[:, 0:WIDTH] * SCALE, tq)
    qb = _block_diag_queries(u[:, 4 * WIDTH:5 * WIDTH] * SCALE, tq)
    rq = lax.broadcasted_iota(jnp.int32, (rows, LANES), 0) % tq
    ln = lax.broadcasted_iota(jnp.int32, (rows, LANES), 1)
    pad = jnp.zeros((LANES - tq, WIDTH), _f32)

    @pl.when(i == 0)
    def _():
        carry = jnp.zeros((16, 1), _f32)
        zrows = jnp.zeros((16 - N_FORGET, kc), _f32)
        for ch in range(nch):
            x = jnp.concatenate([lfc_ref[0, :, ch * kc:(ch + 1) * kc], zrows], axis=0)
            cch = _lane_cumsum(x, carry)
            cc_ref[ch] = cch
            carry = cch[:, kc - 1:kc]
        xn = jnp.concatenate([lfn_ref[0], jnp.zeros((16 - N_FORGET, LANES), _f32)], axis=0)
        cnew = _lane_cumsum(xn, carry)
        xnew = _expand_heads(cnew[0:N_HEADS, :], tq)
        xnew_ref[...] = xnew
        cq = jnp.sum(jnp.where(ln == rq, xnew, 0.0), axis=1, keepdims=True)

        knew = jnp.concatenate([u[:, WIDTH:2 * WIDTH], pad], axis=0).astype(_bf16)
        vnew = jnp.concatenate([u[:, 2 * WIDTH:3 * WIDTH], pad], axis=0).astype(_bf16)
        logit = jnp.where(ln <= rq, _nt_dot(qa, knew) + (cq - xnew), NEG_BIG)
        m = jnp.max(logit, axis=1, keepdims=True)
        p = jnp.exp(logit - m)
        m_ref[...] = m
        l_ref[...] = jnp.sum(p, axis=1, keepdims=True)
        acca_ref[...] = jnp.dot(p.astype(_bf16), vnew, preferred_element_type=_f32)

        knew = jnp.concatenate([u[:, 5 * WIDTH:6 * WIDTH], pad], axis=0).astype(_bf16)
        vnew = jnp.concatenate([u[:, 6 * WIDTH:7 * WIDTH], pad], axis=0).astype(_bf16)
        z = _nt_dot(qb, knew)
        msk = ln < rq
        sp = jnp.where(msk, _softplus(z), 0.0)
        r = lax.broadcasted_iota(jnp.int32, (2 * LANES, LANES), 0)
        c = lax.broadcasted_iota(jnp.int32, (2 * LANES, LANES), 1)
        r = jnp.where(r >= LANES, r - LANES, r)
        suffix = jnp.where(r > c, 1.0, 0.0).astype(_bf16)
        within = jnp.dot(_split2_cols(sp), suffix, preferred_element_type=_f32)
        a = jnp.where(msk, jnp.exp(z - sp - within), 0.0)
        accb_ref[...] = jnp.dot(a.astype(_bf16), vnew, preferred_element_type=_f32)
        later_ref[...] = jnp.sum(sp, axis=1, keepdims=True)

    xq = xnew_ref[...]
    cq = jnp.sum(jnp.where(ln == rq, xq, 0.0), axis=1, keepdims=True)
    ck = _expand_heads(cc_ref[i][0:N_HEADS, :], tq)
    logit = _nt_dot(qa, ka_ref[0, 0].astype(_bf16)) + (cq - ck)
    m_old = m_ref[...]
    m_new = jnp.maximum(m_old, jnp.max(logit, axis=1, keepdims=True))
    alpha = jnp.exp(m_old - m_new)
    p = jnp.exp(logit - m_new)
    m_ref[...] = m_new
    l_ref[...] = alpha * l_ref[...] + jnp.sum(p, axis=1, keepdims=True)
    acca_ref[...] = alpha * acca_ref[...] + jnp.dot(
        p.astype(_bf16), va_ref[0, 0].astype(_bf16), preferred_element_type=_f32)

    z = _nt_dot(qb, kb_ref[0, 0].astype(_bf16))
    sp = _softplus(z)
    within = jnp.dot(_split2_cols(sp), tri_ref[...], preferred_element_type=_f32)
    a = jnp.exp(z - sp - within - later_ref[...])
    accb_ref[...] = accb_ref[...] + jnp.dot(
        a.astype(_bf16), vb_ref[0, 0].astype(_bf16), preferred_element_type=_f32)
    later_ref[...] = later_ref[...] + jnp.sum(sp, axis=1, keepdims=True)

    @pl.when(i == nch - 1)
    def _():
        oa_ref[0] = _pick_heads(acca_ref[...] / l_ref[...], tq)
        ob_ref[0] = _pick_heads(accb_ref[...], tq)


def _sample_attention(layer, u, lf_new, lf_cache_t, ck_a, cv_a, ck_b, cv_b, tri):
    nb, tq, _ = u.shape
    past = ck_a.shape[2]
    kc = SAMPLE_CHUNK
    nch = past // kc
    rows = N_HEADS * tq
    fwd = lambda bi, i: (layer, bi, i, 0)
    rev = lambda bi, i: (layer, bi, nch - 1 - i, 0)
    per_b = lambda bi, i: (bi, 0, 0)
    return pl.pallas_call(
        functools.partial(_sample_attn_kernel, tq=tq, nch=nch),
        out_shape=(jax.ShapeDtypeStruct((nb, tq, WIDTH), _f32),
                   jax.ShapeDtypeStruct((nb, tq, WIDTH), _f32)),
        grid=(nb, nch),
        in_specs=[
            pl.BlockSpec((1, tq, u.shape[2]), per_b),
            pl.BlockSpec((1, N_FORGET, LANES), per_b),
            pl.BlockSpec((1, N_FORGET, past), per_b),
            pl.BlockSpec((1, 1, kc, WIDTH), fwd),
            pl.BlockSpec((1, 1, kc, WIDTH), fwd),
            pl.BlockSpec((1, 1, kc, WIDTH), rev),
            pl.BlockSpec((1, 1, kc, WIDTH), rev),
            pl.BlockSpec(tri.shape, lambda bi, i: (0, 0)),
        ],
        out_specs=(pl.BlockSpec((1, tq, WIDTH), per_b), pl.BlockSpec((1, tq, WIDTH), per_b)),
        scratch_shapes=[
            pltpu.VMEM((nch, 16, kc), _f32),
            pltpu.VMEM((rows, LANES), _f32),
            pltpu.VMEM((rows, 1), _f32),
            pltpu.VMEM((rows, 1), _f32),
            pltpu.VMEM((rows, WIDTH), _f32),
            pltpu.VMEM((rows, 1), _f32),
            pltpu.VMEM((rows, WIDTH), _f32),
        ],
        compiler_params=pltpu.CompilerParams(
            dimension_semantics=("arbitrary", "arbitrary"), vmem_limit_bytes=VMEM_LIMIT),
        name="sample_attention",
    )(u, lf_new, lf_cache_t, ck_a, cv_a, ck_b, cv_b, tri)


def _split_weights(w):
    f0 = 3 * WIDTH
    w_main = jnp.concatenate([w[:, 0:f0], w[:, f0 + N_FORGET:]], axis=1).astype(_bf16)
    w_f = jnp.pad(w[:, f0:f0 + N_FORGET], ((0, 0), (0, LANES - N_FORGET))).astype(_bf16)
    return w_main, w_f


def kernel(x_prompt, x_sample, cache_fox_k, cache_fox_v, cache_fox_logf, cache_sb_k, cache_sb_v,
           norm_g, w_in, b_f, out_norm_a, out_norm_b, w_out, final_norm_g):
    depth = norm_g.shape[0]
    b, t, d = x_prompt.shape
    nb, tq, _ = x_sample.shape
    past = cache_fox_k.shape[2]
    nkb = t // KEY_BLOCK
    assert t % QUERY_BLOCK == 0 and t % PROJ_ROWS == 0 and past % SAMPLE_CHUNK == 0
    assert tq <= 16 and (nb * tq) % 8 == 0

    kc = SAMPLE_CHUNK
    r = lax.broadcasted_iota(jnp.int32, (2 * kc, kc), 0) % kc
    c = lax.broadcasted_iota(jnp.int32, (2 * kc, kc), 1)
    tri = jnp.where(r > c, 1.0, 0.0).astype(_bf16)
    fg = final_norm_g.reshape(1, d)
    flat = lambda a: a.reshape(depth, nb, past, WIDTH)
    cache_fox_k, cache_fox_v = flat(cache_fox_k), flat(cache_fox_v)
    cache_sb_k, cache_sb_v = flat(cache_sb_k), flat(cache_sb_v)

    xp, xs = x_prompt, x_sample.reshape(1, nb * tq, d)
    prompt_new, sample_new = [], []
    for l in range(depth):
        w_main, w_f = _split_weights(w_in[l])
        g = norm_g[l].reshape(1, d)
        bf = jnp.pad(b_f[l], (0, LANES - N_FORGET)).reshape(1, LANES)
        ga, gb = out_norm_a[l].reshape(1, WIDTH), out_norm_b[l].reshape(1, WIDTH)
        wo = w_out[l].astype(_bf16)
        final = l == depth - 1

        ka, va, kb, vb, logf, ct, qt, kbf, vt, z = _inproj(xp, g, w_main, w_f, bf)
        ct = ct.reshape(b, N_HEADS, nkb, KEY_BLOCK)
        oa = _prompt_attention(_fox_kernel, "fox_attention", qt, kbf, vt, ct, 0)
        ob = _prompt_attention(_sb_kernel, "sb_attention", qt, kbf, vt, None, 1)
        xp = _merge(oa, ob, z, xp, ga, gb, wo, fg, final)
        hd = lambda a: a.reshape(b, t, N_HEADS, HEAD_DIM)
        prompt_new.append((hd(ka), hd(va), logf, hd(kb), hd(vb)))

        u, logf_s, logft_s = _sample_inproj(xs[0], g, w_main, w_f, bf)
        u3 = u.reshape(nb, tq, u.shape[1])
        lf_new = jnp.pad(logft_s.reshape(N_FORGET, nb, tq).transpose(1, 0, 2),
                         ((0, 0), (0, 0), (0, LANES - tq)))
        lf_cache_t = cache_fox_logf[l].transpose(0, 2, 1)
        oa_s, ob_s = _sample_attention(l, u3, lf_new, lf_cache_t,
                                       cache_fox_k, cache_fox_v, cache_sb_k, cache_sb_v, tri)
        z_s = jnp.concatenate([u[:, 3 * WIDTH:4 * WIDTH], u[:, 7 * WIDTH:8 * WIDTH]], axis=1)
        xs = _merge(oa_s.reshape(1, nb * tq, WIDTH), ob_s.reshape(1, nb * tq, WIDTH),
                    z_s.reshape(1, nb * tq, 2 * WIDTH), xs, ga, gb, wo, fg, final)
        hs = lambda a: a.reshape(nb, tq, N_HEADS, HEAD_DIM)
        sample_new.append((hs(u[:, WIDTH:2 * WIDTH]), hs(u[:, 2 * WIDTH:3 * WIDTH]),
                           logf_s.reshape(nb, tq, N_FORGET),
                           hs(u[:, 5 * WIDTH:6 * WIDTH]), hs(u[:, 6 * WIDTH:7 * WIDTH])))

    stack = lambda items, i: jnp.stack([it[i] for it in items])
    return (xp, xs.reshape(nb, tq, d),
            *(stack(prompt_new, i) for i in range(5)),
            *(stack(sample_new, i) for i in range(5)))
```

```python
import functools

import jax
import jax.numpy as jnp
from jax import lax
from jax.experimental import pallas as pl
from jax.experimental.pallas import tpu as pltpu

HEAD_DIM = 64
N_HEADS = 8
WIDTH = N_HEADS * HEAD_DIM
N_FORGET = N_HEADS
EPS = 1e-6
SCALE = HEAD_DIM ** -0.5
LOG2E = 1.4426950408889634
NEG_BIG = -1e30

LANES = 128
KEY_BLOCK = 256
QUERY_BLOCK = 256
SLAB = 256
HEADS_PER_SLAB = SLAB // HEAD_DIM
HEADS_PER_STEP = 8
PROJ_ROWS = 256
SAMPLE_CHUNK = 512
VMEM_LIMIT = 48 * 1024 * 1024

_f32 = jnp.float32
_bf16 = jnp.bfloat16


def _softplus(z):
    neg_abs = lax.bitcast_convert_type(
        lax.bitcast_convert_type(z, jnp.uint32) | jnp.uint32(0x80000000), _f32)
    return jnp.maximum(z, 0.0) + jnp.log(1.0 + jnp.exp(neg_abs))


def _softplus2(zs):
    neg_abs = lax.bitcast_convert_type(
        lax.bitcast_convert_type(zs, jnp.uint32) | jnp.uint32(0x80000000), _f32)
    return jnp.maximum(zs, 0.0) + jnp.log(1.0 + jnp.exp2(neg_abs)) * LOG2E


def _log_sigmoid(x):
    return jnp.minimum(x, 0.0) - jnp.log(1.0 + jnp.exp(-jnp.abs(x)))


def _rmsnorm(x, g):
    return x * lax.rsqrt(jnp.mean(x * x, axis=-1, keepdims=True) + EPS) * g


def _split3(x):
    a1 = x.astype(_bf16)
    r1 = x - a1.astype(_f32)
    a2 = r1.astype(_bf16)
    a3 = (r1 - a2.astype(_f32)).astype(_bf16)
    return a1, a2, a3


def _split2_rows(x):
    hi = lax.bitcast_convert_type(
        lax.bitcast_convert_type(x, jnp.uint32) & jnp.uint32(0xFFFF0000), _f32)
    return jnp.concatenate([hi.astype(_bf16), (x - hi).astype(_bf16)], axis=0)


def _split2_cols(x):
    hi = lax.bitcast_convert_type(
        lax.bitcast_convert_type(x, jnp.uint32) & jnp.uint32(0xFFFF0000), _f32)
    return jnp.concatenate([hi.astype(_bf16), (x - hi).astype(_bf16)], axis=1)


def _lane_cumsum(x, carry):
    n = x.shape[1]
    r = lax.broadcasted_iota(jnp.int32, (n, n), 0)
    c = lax.broadcasted_iota(jnp.int32, (n, n), 1)
    tri = jnp.where(r <= c, 1.0, 0.0).astype(_bf16)
    a1, a2, a3 = _split3(x)
    dot = functools.partial(jnp.dot, preferred_element_type=_f32)
    return carry + (dot(a1, tri) + dot(a2, tri) + dot(a3, tri))


N_STACKED = 5


def _inproj_kernel(x_ref, g_ref, w_ref, wf_ref, bf_ref, *refs, first):
    if not first:
        refs = refs[N_STACKED:]
    kat_ref, vat_ref, kbt_ref, vbt_ref, lft_ref, ct_ref, qt_ref, kbf_ref, vt_ref, z_ref, carry_ref = refs
    tm = x_ref.shape[1]
    h = _rmsnorm(x_ref[0], g_ref[...]).astype(_bf16)

    def seg(i):
        return jnp.dot(h, w_ref[:, i * WIDTH:(i + 1) * WIDTH], preferred_element_type=_f32)

    def put_stacked(ref, value):
        for layer in range(ref.shape[0]):
            ref[layer, 0] = value

    for grp, (kt_ref, vtf_ref) in enumerate(((kat_ref, vat_ref), (kbt_ref, vbt_ref))):
        lo, hi = grp * WIDTH, (grp + 1) * WIDTH
        q = seg(4 * grp + 0) * (SCALE * LOG2E)
        qt_ref[0, lo:hi, :] = q.T.astype(_bf16)
        k = seg(4 * grp + 1)
        put_stacked(kt_ref, k.T)
        kbf_ref[0, :, lo:hi] = k.astype(_bf16)
        vt = seg(4 * grp + 2).T
        put_stacked(vtf_ref, vt)
        for s in range(tm // KEY_BLOCK):
            vt_ref[0, s, lo:hi, :] = vt[:, s * KEY_BLOCK:(s + 1) * KEY_BLOCK].astype(_bf16)
        z_ref[0, :, lo:hi] = seg(4 * grp + 3)

    fa = jnp.dot(h, wf_ref[...], preferred_element_type=_f32) + bf_ref[...]
    logft = _log_sigmoid(fa).T[0:16, :]
    put_stacked(lft_ref, logft[0:N_FORGET, :])

    @pl.when(pl.program_id(1) == 0)
    def _():
        carry_ref[...] = jnp.zeros_like(carry_ref)

    c = _lane_cumsum(logft, carry_ref[:, 0:1])
    ct_ref[0] = c[0:N_FORGET, :] * LOG2E
    carry_ref[...] = jnp.broadcast_to(c[:, tm - 1:tm], carry_ref.shape)


def _inproj(layer, depth, stacked, x, g, w_main, w_f, b_f):
    b, t, d = x.shape
    tm = PROJ_ROWS
    nkb = t // KEY_BLOCK
    first = stacked is None
    row = lambda bi, ti: (bi, ti, 0)
    const2 = lambda bi, ti: (0, 0)
    feat = lambda bi, ti: (bi, 0, ti)
    widths = (WIDTH, WIDTH, WIDTH, WIDTH, N_FORGET)
    slabs = depth if first else 1
    slab = lambda bi, ti: (0 if first else layer, bi, 0, ti)
    out_shape = tuple(jax.ShapeDtypeStruct((depth, b, w, t), _f32) for w in widths) + (
        jax.ShapeDtypeStruct((b, N_FORGET, t), _f32),
        jax.ShapeDtypeStruct((b, 2 * WIDTH, t), _bf16),
        jax.ShapeDtypeStruct((b, t, 2 * WIDTH), _bf16),
        jax.ShapeDtypeStruct((b, nkb, 2 * WIDTH, KEY_BLOCK), _bf16),
        jax.ShapeDtypeStruct((b, t, 2 * WIDTH), _f32),
    )
    out_specs = tuple(pl.BlockSpec((slabs, 1, w, tm), slab) for w in widths) + (
        pl.BlockSpec((1, N_FORGET, tm), feat),
        pl.BlockSpec((1, 2 * WIDTH, tm), feat),
        pl.BlockSpec((1, tm, 2 * WIDTH), row),
        pl.BlockSpec((1, tm // KEY_BLOCK, 2 * WIDTH, KEY_BLOCK), lambda bi, ti: (bi, ti, 0, 0)),
        pl.BlockSpec((1, tm, 2 * WIDTH), row),
    )
    in_specs = [
        pl.BlockSpec((1, tm, d), row),
        pl.BlockSpec((1, d), const2),
        pl.BlockSpec(w_main.shape, const2),
        pl.BlockSpec(w_f.shape, const2),
        pl.BlockSpec((1, LANES), const2),
    ]
    args = [x, g, w_main, w_f, b_f]
    aliases = {}
    if not first:
        in_specs += [pl.BlockSpec(memory_space=pl.ANY)] * N_STACKED
        aliases = {len(args) + i: i for i in range(N_STACKED)}
        args += list(stacked)
    return pl.pallas_call(
        functools.partial(_inproj_kernel, first=first),
        out_shape=out_shape,
        grid=(b, t // tm),
        in_specs=in_specs,
        out_specs=out_specs,
        scratch_shapes=[pltpu.VMEM((16, LANES), _f32)],
        input_output_aliases=aliases,
        compiler_params=pltpu.CompilerParams(
            dimension_semantics=("arbitrary", "arbitrary"), vmem_limit_bytes=VMEM_LIMIT),
        name="prompt_inproj",
    )(*args)


def _head_queries(qt_ref):
    out = []
    for s in range(qt_ref.shape[1] // SLAB):
        qt = qt_ref[0, s * SLAB:(s + 1) * SLAB, :]
        head = lax.broadcasted_iota(jnp.int32, qt.shape, 0) // HEAD_DIM
        out += [jnp.where(head == hh, qt, jnp.zeros_like(qt)) for hh in range(SLAB // HEAD_DIM)]
    return out


_FOX_STAGES = (("scores", (), ("s",)), ("weights", ("s",), ("p", "alpha")), ("values", ("p", "alpha"), ()))
_SB_STAGES = (("scores", (), ("z",)), ("survival", ("z",), ("sp",)), ("suffix_sums", ("sp",), ("w",)),
              ("weights", ("z", "w"), ("a",)), ("values", ("a",), ()))
_FOX_GAP = 4
_SB_GAP = 1


def _intermediate_type(name):
    tile = (KEY_BLOCK, QUERY_BLOCK)
    return {"s": (tile, _f32), "z": (tile, _f32), "p": (tile, _bf16), "sp": (tile, _bf16), "a": (tile, _bf16),
            "alpha": ((1, QUERY_BLOCK), _f32), "w": ((KEY_BLOCK + 16, QUERY_BLOCK), _f32)}[name]


def _handoff_keys(stages, nh, gap):
    keys = []
    for h in range(nh):
        done = min(len(stages), (nh - 1 - h) // gap + 1)
        made = {n for (_, _, w) in stages[:done] for n in w}
        need = {n for (_, r, _) in stages[done:] for n in r}
        keys += [(n, h) for n in sorted(made & need)]
    return keys


class _Block:
    def __init__(self, kb, mask, handoff):
        self.kb, self.mask, self.handoff, self.vals = kb, mask, handoff, {}

    def put(self, name, h, value):
        self.vals[(name, h)] = value
        if (name, h) in self.handoff:
            self.handoff[(name, h)][...] = value

    def get(self, name, h):
        if (name, h) in self.vals:
            return self.vals[(name, h)]
        return self.handoff[(name, h)][...]


class _HeadPipeline:
    def __init__(self, stages, fns, nh, gap):
        self.fns, self.nh, self.gap = [fns[name] for (name, _, _) in stages], nh, gap
        self.tail = gap * (len(stages) - 1)
        assert self.tail <= nh

    def _step(self, blk, step):
        for k, fn in enumerate(self.fns):
            if 0 <= step - self.gap * k < self.nh:
                fn(blk, step - self.gap * k)

    def advance(self, prev, cur):
        for step in range(self.nh):
            if prev is not None and step < self.tail:
                self._step(prev, self.nh + step)
            self._step(cur, step)

    def finish(self, prev):
        for step in range(self.tail):
            self._step(prev, self.nh + step)


def _key_slab(k_ref, kb, h):
    s = h // HEADS_PER_SLAB
    return k_ref[0, pl.ds(pl.multiple_of(kb * KEY_BLOCK, KEY_BLOCK), KEY_BLOCK), s * SLAB:(s + 1) * SLAB]


def _fox_kernel(qt_ref, k_ref, vt_ref, ct_ref, o_ref, m_ref, acc_ref, *handoff_refs):
    j = pl.program_id(2)
    nh = qt_ref.shape[1] // HEAD_DIM
    handoff = dict(zip(_handoff_keys(_FOX_STAGES, nh, _FOX_GAP), handoff_refs))
    qh = _head_queries(qt_ref)
    ndiag = QUERY_BLOCK // KEY_BLOCK
    nfull = ndiag * j
    cq = [jnp.concatenate([ct_ref[0, h, pl.ds(nfull + d, 1), :] for d in range(ndiag)], axis=1)
          for h in range(nh)]
    kio = lax.broadcasted_iota(jnp.int32, (KEY_BLOCK, QUERY_BLOCK), 0)
    qio = lax.broadcasted_iota(jnp.int32, (KEY_BLOCK, QUERY_BLOCK), 1)
    causal = [d * KEY_BLOCK + kio <= qio for d in range(ndiag)]
    ones_rows = (lax.broadcasted_iota(jnp.int32, (16, KEY_BLOCK), 0) == 0).astype(_bf16)

    m_ref[...] = jnp.full(m_ref.shape, NEG_BIG, _f32)
    acc_ref[...] = jnp.zeros(acc_ref.shape, _f32)

    def scores(blk, h):
        s = jnp.dot(_key_slab(k_ref, blk.kb, h), qh[h], preferred_element_type=_f32)
        blk.put("s", h, s if blk.mask is None else jnp.where(blk.mask, s, NEG_BIG))

    def weights(blk, h):
        ck_row = ct_ref[0, h, pl.ds(blk.kb, 1), :]
        ck = jnp.broadcast_to(ck_row, (LANES, KEY_BLOCK)).T
        ck = jnp.concatenate([ck] * (QUERY_BLOCK // LANES), axis=1)
        logit = blk.get("s", h) + (cq[h] - ck)
        m_old = m_ref[h]
        m_new = jnp.maximum(m_old, jnp.max(logit, axis=0, keepdims=True))
        blk.put("alpha", h, jnp.exp2(m_old - m_new))
        blk.put("p", h, jnp.exp2(logit - m_new).astype(_bf16))
        m_ref[h] = m_new

    def values(blk, h):
        vth = jnp.concatenate([vt_ref[0, blk.kb, h * HEAD_DIM:(h + 1) * HEAD_DIM, :], ones_rows], axis=0)
        acc_ref[h] = (blk.get("alpha", h) * acc_ref[h]
                      + jnp.dot(vth, blk.get("p", h), preferred_element_type=_f32))

    pipe = _HeadPipeline(_FOX_STAGES, {"scores": scores, "weights": weights, "values": values}, nh, _FOX_GAP)
    prev = None
    for d in range(ndiag):
        cur = _Block(nfull + d, causal[d], handoff)
        pipe.advance(prev, cur)
        prev = cur

    last_diag = nfull + ndiag - 1

    @pl.loop(0, nfull)
    def _(i):
        in_flight = jnp.where(i == 0, last_diag, nfull - i)
        pipe.advance(_Block(in_flight, None, handoff), _Block(nfull - 1 - i, None, handoff))

    pipe.finish(_Block(jnp.where(nfull == 0, last_diag, 0), None, handoff))
    o = jnp.concatenate([acc_ref[h, 0:HEAD_DIM] / acc_ref[h, HEAD_DIM:HEAD_DIM + 1] for h in range(nh)], axis=0)
    o_ref[0] = o.T


def _sb_kernel(qt_ref, k_ref, vt_ref, o_ref, later_ref, acc_ref, *handoff_refs):
    j = pl.program_id(2)
    nh = qt_ref.shape[1] // HEAD_DIM
    handoff = dict(zip(_handoff_keys(_SB_STAGES, nh, _SB_GAP), handoff_refs))
    qh = _head_queries(qt_ref)
    ndiag = QUERY_BLOCK // KEY_BLOCK
    nfull = ndiag * j
    kio = lax.broadcasted_iota(jnp.int32, (KEY_BLOCK, QUERY_BLOCK), 0)
    qio = lax.broadcasted_iota(jnp.int32, (KEY_BLOCK, QUERY_BLOCK), 1)
    strict = [d * KEY_BLOCK + kio < qio for d in range(ndiag)]
    r = lax.broadcasted_iota(jnp.int32, (KEY_BLOCK + 16, KEY_BLOCK), 0)
    c = lax.broadcasted_iota(jnp.int32, (KEY_BLOCK + 16, KEY_BLOCK), 1)
    suffix = jnp.where((c >= r) | (r == KEY_BLOCK), 1.0, 0.0).astype(_bf16)

    later_ref[...] = jnp.zeros(later_ref.shape, _f32)
    acc_ref[...] = jnp.zeros(acc_ref.shape, _f32)

    def scores(blk, h):
        z = jnp.dot(_key_slab(k_ref, blk.kb, h), qh[h], preferred_element_type=_f32)
        blk.put("z", h, z if blk.mask is None else jnp.where(blk.mask, z, NEG_BIG))

    def survival(blk, h):
        blk.put("sp", h, _softplus2(blk.get("z", h)).astype(_bf16))

    def suffix_sums(blk, h):
        blk.put("w", h, jnp.dot(suffix, blk.get("sp", h), preferred_element_type=_f32))

    def weights(blk, h):
        w = blk.get("w", h)
        later = later_ref[h]
        blk.put("a", h, jnp.exp2(blk.get("z", h) - w[0:KEY_BLOCK] - later).astype(_bf16))
        later_ref[h] = later + w[KEY_BLOCK:KEY_BLOCK + 1]

    def values(blk, h):
        vth = vt_ref[0, blk.kb, h * HEAD_DIM:(h + 1) * HEAD_DIM, :]
        acc_ref[h] = acc_ref[h] + jnp.dot(vth, blk.get("a", h), preferred_element_type=_f32)

    pipe = _HeadPipeline(_SB_STAGES, {"scores": scores, "survival": survival, "suffix_sums": suffix_sums,
                                      "weights": weights, "values": values}, nh, _SB_GAP)
    prev = None
    for d in reversed(range(ndiag)):
        cur = _Block(nfull + d, strict[d], handoff)
        pipe.advance(prev, cur)
        prev = cur

    @pl.loop(0, nfull)
    def _(i):
        pipe.advance(_Block(nfull - i, None, handoff), _Block(nfull - 1 - i, None, handoff))

    pipe.finish(_Block(0, None, handoff))
    o_ref[0] = jnp.concatenate([acc_ref[h] for h in range(nh)], axis=0).T


def _prompt_attention(kernel_fn, name, qt, kbf, vt, ct, group):
    b, _, t = qt.shape
    nkb = t // KEY_BLOCK
    wstep = HEADS_PER_STEP * HEAD_DIM
    nstep = WIDTH // wstep
    off = group * nstep
    in_specs = [
        pl.BlockSpec((1, wstep, QUERY_BLOCK), lambda bi, p, j: (bi, off + p, j)),
        pl.BlockSpec((1, t, wstep), lambda bi, p, j: (bi, 0, off + p)),
        pl.BlockSpec((1, nkb, wstep, KEY_BLOCK), lambda bi, p, j: (bi, 0, off + p, 0)),
    ]
    args = [qt, kbf, vt]
    acc_rows = HEAD_DIM
    if ct is not None:
        in_specs.append(pl.BlockSpec((1, HEADS_PER_STEP, nkb, KEY_BLOCK), lambda bi, p, j: (bi, p, 0, 0)))
        args.append(ct)
        acc_rows += 16
    stages, gap = (_FOX_STAGES, _FOX_GAP) if group == 0 else (_SB_STAGES, _SB_GAP)
    scratch = [pltpu.VMEM((HEADS_PER_STEP, 1, QUERY_BLOCK), _f32),
               pltpu.VMEM((HEADS_PER_STEP, acc_rows, QUERY_BLOCK), _f32)]
    scratch += [pltpu.VMEM(*_intermediate_type(name))
                for (name, _) in _handoff_keys(stages, HEADS_PER_STEP, gap)]
    return pl.pallas_call(
        kernel_fn,
        out_shape=jax.ShapeDtypeStruct((b, t, WIDTH), _f32),
        grid=(b, nstep, t // QUERY_BLOCK),
        in_specs=in_specs,
        out_specs=pl.BlockSpec((1, QUERY_BLOCK, wstep), lambda bi, p, j: (bi, j, p)),
        scratch_shapes=scratch,
        compiler_params=pltpu.CompilerParams(
            dimension_semantics=("arbitrary", "arbitrary", "arbitrary"), vmem_limit_bytes=VMEM_LIMIT),
        name=name,
    )(*args)


def _merge_kernel(oa_ref, ob_ref, z_ref, x_ref, ga_ref, gb_ref, w_ref, fg_ref, out_ref, *, final):
    def gated(o, g, z):
        return (_rmsnorm(o, g) * (z * (1.0 / (1.0 + jnp.exp(-z))))).astype(_bf16)

    z = z_ref[0]
    ya = gated(oa_ref[0], ga_ref[...], z[:, 0:WIDTH])
    yb = gated(ob_ref[0], gb_ref[...], z[:, WIDTH:2 * WIDTH])
    y = (jnp.dot(ya, w_ref[0:WIDTH, :], preferred_element_type=_f32)
         + jnp.dot(yb, w_ref[WIDTH:2 * WIDTH, :], preferred_element_type=_f32))
    xn = x_ref[0] + y
    out_ref[0] = _rmsnorm(xn, fg_ref[...]) if final else xn


def _merge(oa, ob, z, x, ga, gb, w_out, fg, final):
    b, t, d = x.shape
    tm = min(PROJ_ROWS, t)
    row = lambda bi, ti: (bi, ti, 0)
    const2 = lambda bi, ti: (0, 0)
    return pl.pallas_call(
        functools.partial(_merge_kernel, final=final),
        out_shape=jax.ShapeDtypeStruct((b, t, d), _f32),
        grid=(b, t // tm),
        in_specs=[
            pl.BlockSpec((1, tm, WIDTH), row),
            pl.BlockSpec((1, tm, WIDTH), row),
            pl.BlockSpec((1, tm, 2 * WIDTH), row),
            pl.BlockSpec((1, tm, d), row),
            pl.BlockSpec((1, WIDTH), const2),
            pl.BlockSpec((1, WIDTH), const2),
            pl.BlockSpec(w_out.shape, const2),
            pl.BlockSpec((1, d), const2),
        ],
        out_specs=pl.BlockSpec((1, tm, d), row),
        compiler_params=pltpu.CompilerParams(
            dimension_semantics=("arbitrary", "arbitrary"), vmem_limit_bytes=VMEM_LIMIT),
        name="merge_final" if final else "merge",
    )(oa, ob, z, x, ga, gb, w_out, fg)


def _sample_inproj_kernel(x_ref, g_ref, w_ref, wf_ref, bf_ref, u_ref, logf_ref, logft_ref):
    h = _rmsnorm(x_ref[...], g_ref[...]).astype(_bf16)
    u_ref[...] = jnp.dot(h, w_ref[...], preferred_element_type=_f32)
    fa = jnp.dot(h, wf_ref[...], preferred_element_type=_f32) + bf_ref[...]
    logf = _log_sigmoid(fa)
    logf_ref[...] = logf[:, 0:N_FORGET]
    logft_ref[...] = logf.T[0:N_FORGET, :]


def _sample_inproj(x2d, g, w_main, w_f, b_f):
    r, d = x2d.shape
    return pl.pallas_call(
        _sample_inproj_kernel,
        out_shape=(jax.ShapeDtypeStruct((r, w_main.shape[1]), _f32),
                   jax.ShapeDtypeStruct((r, N_FORGET), _f32),
                   jax.ShapeDtypeStruct((N_FORGET, r), _f32)),
        compiler_params=pltpu.CompilerParams(vmem_limit_bytes=VMEM_LIMIT),
        name="sample_inproj",
    )(x2d, g, w_main, w_f, b_f)


def _expand_heads(m, rows_per_head):
    return jnp.concatenate(
        [jnp.broadcast_to(m[h:h + 1, :], (rows_per_head, m.shape[1])) for h in range(N_HEADS)], axis=0)


def _nt_dot(a, b):
    return lax.dot_general(a, b, (((1,), (1,)), ((), ())), preferred_element_type=_f32)


def _head_cols(x, h):
    return x[:, h * HEAD_DIM:(h + 1) * HEAD_DIM]


def _scores_cached(q, kt_ref):
    return jnp.concatenate([jnp.dot(_head_cols(q, h), kt_ref[0, 0, h].astype(_bf16), preferred_element_type=_f32)
                            for h in range(N_HEADS)], axis=0)


def _values_cached(p, vt_ref, tq):
    return jnp.concatenate([_nt_dot(p[h * tq:(h + 1) * tq], vt_ref[0, 0, h].astype(_bf16))
                            for h in range(N_HEADS)], axis=0)


def _scores_new(q, knew):
    return jnp.concatenate([_nt_dot(_head_cols(q, h), _head_cols(knew, h)) for h in range(N_HEADS)], axis=0)


def _values_new(p, vnew, tq):
    return jnp.concatenate([jnp.dot(p[h * tq:(h + 1) * tq], _head_cols(vnew, h), preferred_element_type=_f32)
                            for h in range(N_HEADS)], axis=0)


def _heads_to_lanes(o, tq):
    return jnp.concatenate([o[h * tq:(h + 1) * tq] for h in range(N_HEADS)], axis=1)


def _sample_attn_kernel(u_ref, lfn_ref, lfc_ref, ka_ref, va_ref, kb_ref, vb_ref, tri_ref,
                        oa_ref, ob_ref,
                        cc_ref, xnew_ref, m_ref, l_ref, acca_ref, later_ref, accb_ref, *, tq, nch):
    i = pl.program_id(1)
    kc = SAMPLE_CHUNK
    rows = N_HEADS * tq
    u = u_ref[0]
    qa = (u[:, 0:WIDTH] * SCALE).astype(_bf16)
    qb = (u[:, 4 * WIDTH:5 * WIDTH] * SCALE).astype(_bf16)
    rq = lax.broadcasted_iota(jnp.int32, (rows, LANES), 0) % tq
    ln = lax.broadcasted_iota(jnp.int32, (rows, LANES), 1)
    pad = jnp.zeros((LANES - tq, WIDTH), _f32)

    @pl.when(i == 0)
    def _():
        carry = jnp.zeros((16, 1), _f32)
        zrows = jnp.zeros((16 - N_FORGET, kc), _f32)
        for ch in range(nch):
            x = jnp.concatenate([lfc_ref[0, :, ch * kc:(ch + 1) * kc], zrows], axis=0)
            cch = _lane_cumsum(x, carry)
            cc_ref[ch] = cch
            carry = cch[:, kc - 1:kc]
        xn = jnp.concatenate([lfn_ref[0], jnp.zeros((16 - N_FORGET, LANES), _f32)], axis=0)
        cnew = _lane_cumsum(xn, carry)
        xnew = _expand_heads(cnew[0:N_HEADS, :], tq)
        xnew_ref[...] = xnew
        cq = jnp.sum(jnp.where(ln == rq, xnew, 0.0), axis=1, keepdims=True)

        knew = jnp.concatenate([u[:, WIDTH:2 * WIDTH], pad], axis=0).astype(_bf16)
        vnew = jnp.concatenate([u[:, 2 * WIDTH:3 * WIDTH], pad], axis=0).astype(_bf16)
        logit = jnp.where(ln <= rq, _scores_new(qa, knew) + (cq - xnew), NEG_BIG)
        m = jnp.max(logit, axis=1, keepdims=True)
        p = jnp.exp(logit - m)
        m_ref[...] = m
        l_ref[...] = jnp.sum(p, axis=1, keepdims=True)
        acca_ref[...] = _values_new(p.astype(_bf16), vnew, tq)

        knew = jnp.concatenate([u[:, 5 * WIDTH:6 * WIDTH], pad], axis=0).astype(_bf16)
        vnew = jnp.concatenate([u[:, 6 * WIDTH:7 * WIDTH], pad], axis=0).astype(_bf16)
        z = _scores_new(qb, knew)
        msk = ln < rq
        sp = jnp.where(msk, _softplus(z), 0.0)
        r = lax.broadcasted_iota(jnp.int32, (2 * LANES, LANES), 0)
        c = lax.broadcasted_iota(jnp.int32, (2 * LANES, LANES), 1)
        r = jnp.where(r >= LANES, r - LANES, r)
        suffix = jnp.where(r > c, 1.0, 0.0).astype(_bf16)
        within = jnp.dot(_split2_cols(sp), suffix, preferred_element_type=_f32)
        a = jnp.where(msk, jnp.exp(z - sp - within), 0.0)
        accb_ref[...] = _values_new(a.astype(_bf16), vnew, tq)
        later_ref[...] = jnp.sum(sp, axis=1, keepdims=True)

    xq = xnew_ref[...]
    cq = jnp.sum(jnp.where(ln == rq, xq, 0.0), axis=1, keepdims=True)
    ck = _expand_heads(cc_ref[i][0:N_HEADS, :], tq)
    logit = _scores_cached(qa, ka_ref) + (cq - ck)
    m_old = m_ref[...]
    m_new = jnp.maximum(m_old, jnp.max(logit, axis=1, keepdims=True))
    alpha = jnp.exp(m_old - m_new)
    p = jnp.exp(logit - m_new)
    m_ref[...] = m_new
    l_ref[...] = alpha * l_ref[...] + jnp.sum(p, axis=1, keepdims=True)
    acca_ref[...] = alpha * acca_ref[...] + _values_cached(p.astype(_bf16), va_ref, tq)

    z = _scores_cached(qb, kb_ref)
    sp = _softplus(z)
    within = jnp.dot(_split2_cols(sp), tri_ref[...], preferred_element_type=_f32)
    a = jnp.exp(z - sp - within - later_ref[...])
    accb_ref[...] = accb_ref[...] + _values_cached(a.astype(_bf16), vb_ref, tq)
    later_ref[...] = later_ref[...] + jnp.sum(sp, axis=1, keepdims=True)

    @pl.when(i == nch - 1)
    def _():
        oa_ref[0] = _heads_to_lanes(acca_ref[...] / l_ref[...], tq)
        ob_ref[0] = _heads_to_lanes(accb_ref[...], tq)


def _sample_attention(layer, u, lf_new, lf_cache_t, ck_a, cv_a, ck_b, cv_b, tri):
    nb, tq, _ = u.shape
    past = ck_a.shape[4]
    kc = SAMPLE_CHUNK
    nch = past // kc
    rows = N_HEADS * tq
    chunk = (1, 1, N_HEADS, HEAD_DIM, kc)
    fwd = lambda bi, i: (layer, bi, 0, 0, i)
    rev = lambda bi, i: (layer, bi, 0, 0, nch - 1 - i)
    per_b = lambda bi, i: (bi, 0, 0)
    return pl.pallas_call(
        functools.partial(_sample_attn_kernel, tq=tq, nch=nch),
        out_shape=(jax.ShapeDtypeStruct((nb, tq, WIDTH), _f32),
                   jax.ShapeDtypeStruct((nb, tq, WIDTH), _f32)),
        grid=(nb, nch),
        in_specs=[
            pl.BlockSpec((1, tq, u.shape[2]), per_b),
            pl.BlockSpec((1, N_FORGET, LANES), per_b),
            pl.BlockSpec((1, N_FORGET, past), per_b),
            pl.BlockSpec(chunk, fwd),
            pl.BlockSpec(chunk, fwd),
            pl.BlockSpec(chunk, rev),
            pl.BlockSpec(chunk, rev),
            pl.BlockSpec(tri.shape, lambda bi, i: (0, 0)),
        ],
        out_specs=(pl.BlockSpec((1, tq, WIDTH), per_b), pl.BlockSpec((1, tq, WIDTH), per_b)),
        scratch_shapes=[
            pltpu.VMEM((nch, 16, kc), _f32),
            pltpu.VMEM((rows, LANES), _f32),
            pltpu.VMEM((rows, 1), _f32),
            pltpu.VMEM((rows, 1), _f32),
            pltpu.VMEM((rows, HEAD_DIM), _f32),
            pltpu.VMEM((rows, 1), _f32),
            pltpu.VMEM((rows, HEAD_DIM), _f32),
        ],
        compiler_params=pltpu.CompilerParams(
            dimension_semantics=("arbitrary", "arbitrary"), vmem_limit_bytes=VMEM_LIMIT),
        name="sample_attention",
    )(u, lf_new, lf_cache_t, ck_a, cv_a, ck_b, cv_b, tri)


def _split_weights(w):
    f0 = 3 * WIDTH
    w_main = jnp.concatenate([w[:, 0:f0], w[:, f0 + N_FORGET:]], axis=1).astype(_bf16)
    w_f = jnp.pad(w[:, f0:f0 + N_FORGET], ((0, 0), (0, LANES - N_FORGET))).astype(_bf16)
    return w_main, w_f


def kernel(x_prompt, x_sample, cache_fox_k, cache_fox_v, cache_fox_logf, cache_sb_k, cache_sb_v,
           norm_g, w_in, b_f, out_norm_a, out_norm_b, w_out, final_norm_g):
    depth = norm_g.shape[0]
    b, t, d = x_prompt.shape
    nb, tq, _ = x_sample.shape
    past = cache_fox_k.shape[2]
    nkb = t // KEY_BLOCK
    assert t % QUERY_BLOCK == 0 and t % PROJ_ROWS == 0 and past % SAMPLE_CHUNK == 0
    assert tq <= 16 and (nb * tq) % 8 == 0

    kc = SAMPLE_CHUNK
    r = lax.broadcasted_iota(jnp.int32, (2 * kc, kc), 0) % kc
    c = lax.broadcasted_iota(jnp.int32, (2 * kc, kc), 1)
    tri = jnp.where(r > c, 1.0, 0.0).astype(_bf16)
    fg = final_norm_g.reshape(1, d)
    time_minor = lambda a: a.transpose(0, 1, 3, 4, 2)
    cache_fox_k, cache_fox_v = time_minor(cache_fox_k), time_minor(cache_fox_v)
    cache_sb_k, cache_sb_v = time_minor(cache_sb_k), time_minor(cache_sb_v)

    xp, xs = x_prompt, x_sample.reshape(1, nb * tq, d)
    stacked, sample_new = None, []
    for l in range(depth):
        w_main, w_f = _split_weights(w_in[l])
        g = norm_g[l].reshape(1, d)
        bf = jnp.pad(b_f[l], (0, LANES - N_FORGET)).reshape(1, LANES)
        ga, gb = out_norm_a[l].reshape(1, WIDTH), out_norm_b[l].reshape(1, WIDTH)
        wo = w_out[l].astype(_bf16)
        final = l == depth - 1

        outs = _inproj(l, depth, stacked, xp, g, w_main, w_f, bf)
        stacked, (ct, qt, kbf, vt, z) = outs[:N_STACKED], outs[N_STACKED:]
        ct = ct.reshape(b, N_HEADS, nkb, KEY_BLOCK)
        oa = _prompt_attention(_fox_kernel, "fox_attention", qt, kbf, vt, ct, 0)
        ob = _prompt_attention(_sb_kernel, "sb_attention", qt, kbf, vt, None, 1)
        xp = _merge(oa, ob, z, xp, ga, gb, wo, fg, final)

        u, logf_s, logft_s = _sample_inproj(xs[0], g, w_main, w_f, bf)
        u3 = u.reshape(nb, tq, u.shape[1])
        lf_new = jnp.pad(logft_s.reshape(N_FORGET, nb, tq).transpose(1, 0, 2),
                         ((0, 0), (0, 0), (0, LANES - tq)))
        lf_cache_t = cache_fox_logf[l].transpose(0, 2, 1)
        oa_s, ob_s = _sample_attention(l, u3, lf_new, lf_cache_t,
                                       cache_fox_k, cache_fox_v, cache_sb_k, cache_sb_v, tri)
        z_s = jnp.concatenate([u[:, 3 * WIDTH:4 * WIDTH], u[:, 7 * WIDTH:8 * WIDTH]], axis=1)
        xs = _merge(oa_s.reshape(1, nb * tq, WIDTH), ob_s.reshape(1, nb * tq, WIDTH),
                    z_s.reshape(1, nb * tq, 2 * WIDTH), xs, ga, gb, wo, fg, final)
        hs = lambda a: a.reshape(nb, tq, N_HEADS, HEAD_DIM)
        sample_new.append((hs(u[:, WIDTH:2 * WIDTH]), hs(u[:, 2 * WIDTH:3 * WIDTH]),
                           logf_s.reshape(nb, tq, N_FORGET),
                           hs(u[:, 5 * WIDTH:6 * WIDTH]), hs(u[:, 6 * WIDTH:7 * WIDTH])))

    heads = lambda a: a.reshape(depth, b, N_HEADS, HEAD_DIM, t).transpose(0, 1, 4, 2, 3)
    kat, vat, kbt, vbt, lft = stacked
    stack = lambda items, i: jnp.stack([it[i] for it in items])
    return (xp, xs.reshape(nb, tq, d),
            heads(kat), heads(vat), lft.transpose(0, 1, 3, 2), heads(kbt), heads(vbt),
            *(stack(sample_new, i) for i in range(5)))
```

```python
import functools

import jax
import jax.numpy as jnp
from jax import lax
from jax.experimental import pallas as pl
from jax.experimental.pallas import tpu as pltpu

HEAD_DIM = 64
N_HEADS = 8
WIDTH = N_HEADS * HEAD_DIM
N_FORGET = N_HEADS
EPS = 1e-6
SCALE = HEAD_DIM ** -0.5
LOG2E = 1.4426950408889634
NEG_BIG = -1e30
UNDERFLOW_EXP2 = 150.0

LANES = 128
KEY_BLOCK = 256
QUERY_BLOCK = 256
SLAB = 256
HEADS_PER_SLAB = SLAB // HEAD_DIM
HEADS_PER_STEP = 8
PROJ_ROWS = 256
SAMPLE_CHUNK = 512
VMEM_LIMIT = 48 * 1024 * 1024

_f32 = jnp.float32
_bf16 = jnp.bfloat16


def _softplus(z):
    neg_abs = lax.bitcast_convert_type(
        lax.bitcast_convert_type(z, jnp.uint32) | jnp.uint32(0x80000000), _f32)
    return jnp.maximum(z, 0.0) + jnp.log(1.0 + jnp.exp(neg_abs))


def _softplus2(zs):
    neg_abs = lax.bitcast_convert_type(
        lax.bitcast_convert_type(zs, jnp.uint32) | jnp.uint32(0x80000000), _f32)
    return jnp.maximum(zs, 0.0) + jnp.log(1.0 + jnp.exp2(neg_abs)) * LOG2E


def _log_sigmoid(x):
    return jnp.minimum(x, 0.0) - jnp.log(1.0 + jnp.exp(-jnp.abs(x)))


def _rmsnorm(x, g):
    return x * lax.rsqrt(jnp.mean(x * x, axis=-1, keepdims=True) + EPS) * g


def _split3(x):
    a1 = x.astype(_bf16)
    r1 = x - a1.astype(_f32)
    a2 = r1.astype(_bf16)
    a3 = (r1 - a2.astype(_f32)).astype(_bf16)
    return a1, a2, a3


def _split2_rows(x):
    hi = lax.bitcast_convert_type(
        lax.bitcast_convert_type(x, jnp.uint32) & jnp.uint32(0xFFFF0000), _f32)
    return jnp.concatenate([hi.astype(_bf16), (x - hi).astype(_bf16)], axis=0)


def _split2_cols(x):
    hi = lax.bitcast_convert_type(
        lax.bitcast_convert_type(x, jnp.uint32) & jnp.uint32(0xFFFF0000), _f32)
    return jnp.concatenate([hi.astype(_bf16), (x - hi).astype(_bf16)], axis=1)


def _lane_cumsum(x, carry):
    n = x.shape[1]
    r = lax.broadcasted_iota(jnp.int32, (n, n), 0)
    c = lax.broadcasted_iota(jnp.int32, (n, n), 1)
    tri = jnp.where(r <= c, 1.0, 0.0).astype(_bf16)
    a1, a2, a3 = _split3(x)
    dot = functools.partial(jnp.dot, preferred_element_type=_f32)
    return carry + (dot(a1, tri) + dot(a2, tri) + dot(a3, tri))


N_STACKED = 5


def _inproj_kernel(x_ref, g_ref, w_ref, wf_ref, bf_ref, *refs, first):
    if not first:
        refs = refs[N_STACKED:]
    kat_ref, vat_ref, kbt_ref, vbt_ref, lft_ref, ct_ref, qt_ref, kbf_ref, vt_ref, z_ref, carry_ref = refs
    tm = x_ref.shape[1]
    h = _rmsnorm(x_ref[0], g_ref[...]).astype(_bf16)

    def seg(i):
        return jnp.dot(h, w_ref[:, i * WIDTH:(i + 1) * WIDTH], preferred_element_type=_f32)

    def put_stacked(ref, value):
        for layer in range(ref.shape[0]):
            ref[layer, 0] = value

    for grp, (kt_ref, vtf_ref) in enumerate(((kat_ref, vat_ref), (kbt_ref, vbt_ref))):
        lo, hi = grp * WIDTH, (grp + 1) * WIDTH
        q = seg(4 * grp + 0) * (SCALE * LOG2E)
        qt_ref[0, lo:hi, :] = q.T.astype(_bf16)
        k = seg(4 * grp + 1)
        put_stacked(kt_ref, k.T)
        kbf_ref[0, :, lo:hi] = k.astype(_bf16)
        vt = seg(4 * grp + 2).T
        put_stacked(vtf_ref, vt)
        for s in range(tm // KEY_BLOCK):
            vt_ref[0, s, lo:hi, :] = vt[:, s * KEY_BLOCK:(s + 1) * KEY_BLOCK].astype(_bf16)
        z_ref[0, :, lo:hi] = seg(4 * grp + 3).astype(z_ref.dtype)

    fa = jnp.dot(h, wf_ref[...], preferred_element_type=_f32) + bf_ref[...]
    logft = _log_sigmoid(fa).T[0:16, :]
    put_stacked(lft_ref, logft[0:N_FORGET, :])

    @pl.when(pl.program_id(1) == 0)
    def _():
        carry_ref[...] = jnp.zeros_like(carry_ref)

    c = _lane_cumsum(logft, carry_ref[:, 0:1])
    ct_ref[0] = c[0:N_FORGET, :] * LOG2E
    carry_ref[...] = jnp.broadcast_to(c[:, tm - 1:tm], carry_ref.shape)


def _inproj(layer, depth, stacked, x, g, w_main, w_f, b_f):
    b, t, d = x.shape
    tm = PROJ_ROWS
    nkb = t // KEY_BLOCK
    first = stacked is None
    row = lambda bi, ti: (bi, ti, 0)
    const2 = lambda bi, ti: (0, 0)
    feat = lambda bi, ti: (bi, 0, ti)
    widths = (WIDTH, WIDTH, WIDTH, WIDTH, N_FORGET)
    slabs = depth if first else 1
    slab = lambda bi, ti: (0 if first else layer, bi, 0, ti)
    out_shape = tuple(jax.ShapeDtypeStruct((depth, b, w, t), _f32) for w in widths) + (
        jax.ShapeDtypeStruct((b, N_FORGET, t), _f32),
        jax.ShapeDtypeStruct((b, 2 * WIDTH, t), _bf16),
        jax.ShapeDtypeStruct((b, t, 2 * WIDTH), _bf16),
        jax.ShapeDtypeStruct((b, nkb, 2 * WIDTH, KEY_BLOCK), _bf16),
        jax.ShapeDtypeStruct((b, t, 2 * WIDTH), _bf16),
    )
    out_specs = tuple(pl.BlockSpec((slabs, 1, w, tm), slab) for w in widths) + (
        pl.BlockSpec((1, N_FORGET, tm), feat),
        pl.BlockSpec((1, 2 * WIDTH, tm), feat),
        pl.BlockSpec((1, tm, 2 * WIDTH), row),
        pl.BlockSpec((1, tm // KEY_BLOCK, 2 * WIDTH, KEY_BLOCK), lambda bi, ti: (bi, ti, 0, 0)),
        pl.BlockSpec((1, tm, 2 * WIDTH), row),
    )
    in_specs = [
        pl.BlockSpec((1, tm, d), row),
        pl.BlockSpec((1, d), const2),
        pl.BlockSpec(w_main.shape, const2),
        pl.BlockSpec(w_f.shape, const2),
        pl.BlockSpec((1, LANES), const2),
    ]
    args = [x, g, w_main, w_f, b_f]
    aliases = {}
    if not first:
        in_specs += [pl.BlockSpec(memory_space=pl.ANY)] * N_STACKED
        aliases = {len(args) + i: i for i in range(N_STACKED)}
        args += list(stacked)
    return pl.pallas_call(
        functools.partial(_inproj_kernel, first=first),
        out_shape=out_shape,
        grid=(b, t // tm),
        in_specs=in_specs,
        out_specs=out_specs,
        scratch_shapes=[pltpu.VMEM((16, LANES), _f32)],
        input_output_aliases=aliases,
        compiler_params=pltpu.CompilerParams(
            dimension_semantics=("arbitrary", "arbitrary"), vmem_limit_bytes=VMEM_LIMIT),
        name="prompt_inproj",
    )(*args)


def _head_queries(qt_ref):
    out = []
    for s in range(qt_ref.shape[1] // SLAB):
        qt = qt_ref[0, s * SLAB:(s + 1) * SLAB, :]
        head = lax.broadcasted_iota(jnp.int32, qt.shape, 0) // HEAD_DIM
        out += [jnp.where(head == hh, qt, jnp.zeros_like(qt)) for hh in range(SLAB // HEAD_DIM)]
    return out


_FOX_STAGES = (("scores", (), ("s",)), ("weights", ("s",), ("p", "alpha")), ("values", ("p", "alpha"), ()))
_SB_STAGES = (("scores", (), ("z",)), ("survival", ("z",), ("sp",)), ("suffix_sums", ("sp",), ("w",)),
              ("weights", ("z", "w"), ("a",)), ("values", ("a",), ()))
_FOX_GAP = 4
_SB_GAP = 1


def _intermediate_type(name):
    tile = (KEY_BLOCK, QUERY_BLOCK)
    return {"s": (tile, _f32), "z": (tile, _f32), "p": (tile, _bf16), "sp": (tile, _bf16), "a": (tile, _bf16),
            "alpha": ((1, QUERY_BLOCK), _f32), "w": ((KEY_BLOCK + 16, QUERY_BLOCK), _f32)}[name]


def _handoff_keys(stages, nh, gap):
    keys = []
    for h in range(nh):
        done = min(len(stages), (nh - 1 - h) // gap + 1)
        made = {n for (_, _, w) in stages[:done] for n in w}
        need = {n for (_, r, _) in stages[done:] for n in r}
        keys += [(n, h) for n in sorted(made & need)]
    return keys


class _Block:
    def __init__(self, kb, mask, handoff):
        self.kb, self.mask, self.handoff, self.vals = kb, mask, handoff, {}

    def put(self, name, h, value):
        self.vals[(name, h)] = value
        if (name, h) in self.handoff:
            self.handoff[(name, h)][...] = value

    def get(self, name, h):
        if (name, h) in self.vals:
            return self.vals[(name, h)]
        return self.handoff[(name, h)][...]


class _HeadPipeline:
    def __init__(self, stages, fns, nh, gap):
        self.fns, self.nh, self.gap = [fns[name] for (name, _, _) in stages], nh, gap
        self.tail = gap * (len(stages) - 1)
        assert self.tail <= nh

    def _step(self, blk, step):
        for k, fn in enumerate(self.fns):
            if 0 <= step - self.gap * k < self.nh:
                fn(blk, step - self.gap * k)

    def advance(self, prev, cur):
        for step in range(self.nh):
            if prev is not None and step < self.tail:
                self._step(prev, self.nh + step)
            self._step(cur, step)

    def finish(self, prev):
        for step in range(self.tail):
            self._step(prev, self.nh + step)


def _key_slab(k_ref, kb, h):
    s = h // HEADS_PER_SLAB
    return k_ref[0, pl.ds(pl.multiple_of(kb * KEY_BLOCK, KEY_BLOCK), KEY_BLOCK), s * SLAB:(s + 1) * SLAB]


def _fox_kernel(qt_ref, k_ref, vt_ref, ct_ref, o_ref, m_ref, acc_ref, *handoff_refs):
    j = pl.program_id(2)
    nh = qt_ref.shape[1] // HEAD_DIM
    handoff = dict(zip(_handoff_keys(_FOX_STAGES, nh, _FOX_GAP), handoff_refs))
    qh = _head_queries(qt_ref)
    ndiag = QUERY_BLOCK // KEY_BLOCK
    nfull = ndiag * j
    cq = [jnp.concatenate([ct_ref[0, h, pl.ds(nfull + d, 1), :] for d in range(ndiag)], axis=1)
          for h in range(nh)]
    kio = lax.broadcasted_iota(jnp.int32, (KEY_BLOCK, QUERY_BLOCK), 0)
    qio = lax.broadcasted_iota(jnp.int32, (KEY_BLOCK, QUERY_BLOCK), 1)
    causal = [d * KEY_BLOCK + kio <= qio for d in range(ndiag)]
    ones_rows = (lax.broadcasted_iota(jnp.int32, (16, KEY_BLOCK), 0) == 0).astype(_bf16)

    m_ref[...] = jnp.full(m_ref.shape, NEG_BIG, _f32)
    acc_ref[...] = jnp.zeros(acc_ref.shape, _f32)

    def scores(blk, h):
        s = jnp.dot(_key_slab(k_ref, blk.kb, h), qh[h], preferred_element_type=_f32)
        blk.put("s", h, s if blk.mask is None else jnp.where(blk.mask, s, NEG_BIG))

    def weights(blk, h):
        ck_row = ct_ref[0, h, pl.ds(blk.kb, 1), :]
        ck = jnp.broadcast_to(ck_row, (LANES, KEY_BLOCK)).T
        ck = jnp.concatenate([ck] * (QUERY_BLOCK // LANES), axis=1)
        logit = blk.get("s", h) + (cq[h] - ck)
        m_old = m_ref[h]
        m_new = jnp.maximum(m_old, jnp.max(logit, axis=0, keepdims=True))
        blk.put("alpha", h, jnp.exp2(m_old - m_new))
        blk.put("p", h, jnp.exp2(logit - m_new).astype(_bf16))
        m_ref[h] = m_new

    def values(blk, h):
        vth = jnp.concatenate([vt_ref[0, blk.kb, h * HEAD_DIM:(h + 1) * HEAD_DIM, :], ones_rows], axis=0)
        acc_ref[h] = (blk.get("alpha", h) * acc_ref[h]
                      + jnp.dot(vth, blk.get("p", h), preferred_element_type=_f32))

    pipe = _HeadPipeline(_FOX_STAGES, {"scores": scores, "weights": weights, "values": values}, nh, _FOX_GAP)
    prev = None
    for d in range(ndiag):
        cur = _Block(nfull + d, causal[d], handoff)
        pipe.advance(prev, cur)
        prev = cur

    last_diag = nfull + ndiag - 1

    @pl.loop(0, nfull)
    def _(i):
        in_flight = jnp.where(i == 0, last_diag, nfull - i)
        pipe.advance(_Block(in_flight, None, handoff), _Block(nfull - 1 - i, None, handoff))

    pipe.finish(_Block(jnp.where(nfull == 0, last_diag, 0), None, handoff))
    o = jnp.concatenate([acc_ref[h, 0:HEAD_DIM] / acc_ref[h, HEAD_DIM:HEAD_DIM + 1] for h in range(nh)], axis=0)
    o_ref[0] = o.T.astype(o_ref.dtype)


def _sb_kernel(qt_ref, k_ref, vt_ref, o_ref, later_ref, acc_ref, kmax_ref, bound_ref, *handoff_refs):
    j = pl.program_id(2)
    nh = qt_ref.shape[1] // HEAD_DIM
    handoff = dict(zip(_handoff_keys(_SB_STAGES, nh, _SB_GAP), handoff_refs))
    qh = _head_queries(qt_ref)
    ndiag = QUERY_BLOCK // KEY_BLOCK
    nfull = ndiag * j
    kio = lax.broadcasted_iota(jnp.int32, (KEY_BLOCK, QUERY_BLOCK), 0)
    qio = lax.broadcasted_iota(jnp.int32, (KEY_BLOCK, QUERY_BLOCK), 1)
    strict = [d * KEY_BLOCK + kio < qio for d in range(ndiag)]
    r = lax.broadcasted_iota(jnp.int32, (KEY_BLOCK + 16, KEY_BLOCK), 0)
    c = lax.broadcasted_iota(jnp.int32, (KEY_BLOCK + 16, KEY_BLOCK), 1)
    suffix = jnp.where((c >= r) | (r == KEY_BLOCK), 1.0, 0.0).astype(_bf16)

    later_ref[...] = jnp.zeros(later_ref.shape, _f32)
    acc_ref[...] = jnp.zeros(acc_ref.shape, _f32)

    def scores(blk, h):
        z = jnp.dot(_key_slab(k_ref, blk.kb, h), qh[h], preferred_element_type=_f32)
        blk.put("z", h, z if blk.mask is None else jnp.where(blk.mask, z, NEG_BIG))

    def survival(blk, h):
        blk.put("sp", h, _softplus2(blk.get("z", h)).astype(_bf16))

    def suffix_sums(blk, h):
        blk.put("w", h, jnp.dot(suffix, blk.get("sp", h), preferred_element_type=_f32))

    def weights(blk, h):
        w = blk.get("w", h)
        later = later_ref[h]
        blk.put("a", h, jnp.exp2(blk.get("z", h) - w[0:KEY_BLOCK] - later).astype(_bf16))
        later_ref[h] = later + w[KEY_BLOCK:KEY_BLOCK + 1]

    def values(blk, h):
        vth = vt_ref[0, blk.kb, h * HEAD_DIM:(h + 1) * HEAD_DIM, :]
        acc_ref[h] = acc_ref[h] + jnp.dot(vth, blk.get("a", h), preferred_element_type=_f32)

    pipe = _HeadPipeline(_SB_STAGES, {"scores": scores, "survival": survival, "suffix_sums": suffix_sums,
                                      "weights": weights, "values": values}, nh, _SB_GAP)
    prev = None
    for d in reversed(range(ndiag)):
        cur = _Block(nfull + d, strict[d], handoff)
        pipe.advance(prev, cur)
        prev = cur

    @pl.when(j == 0)
    def _():
        lane_head = lax.broadcasted_iota(jnp.int32, (SLAB, LANES), 0) // HEAD_DIM
        group = (lane_head == lax.broadcasted_iota(jnp.int32, (SLAB, LANES), 1)).astype(_bf16)
        for s in range(nh // HEADS_PER_SLAB):
            k = k_ref[0, :, s * SLAB:(s + 1) * SLAB].astype(_f32)
            norms2 = jnp.dot((k * k).astype(_bf16), group, preferred_element_type=_f32)
            kmax_ref[s] = jnp.max(norms2, axis=0, keepdims=True)
    lane = lax.broadcasted_iota(jnp.int32, (1, LANES), 1)
    for h in range(nh):
        kmax2 = jnp.max(jnp.where(lane == h % HEADS_PER_SLAB, kmax_ref[h // HEADS_PER_SLAB], 0.0),
                        axis=1, keepdims=True)
        qf = qt_ref[0, h * HEAD_DIM:(h + 1) * HEAD_DIM, :].astype(_f32)
        qn2 = jnp.sum(qf * qf, axis=0, keepdims=True)
        bound_ref[h] = 1.05 * jnp.sqrt(qn2 * kmax2) + UNDERFLOW_EXP2

    def unfinished():
        slack = later_ref[0] - bound_ref[0]
        for h in range(1, nh):
            slack = jnp.minimum(slack, later_ref[h] - bound_ref[h])
        return jnp.min(slack) < 0.0

    def body(carry):
        i, _ = carry
        pipe.advance(_Block(nfull - i, None, handoff), _Block(nfull - 1 - i, None, handoff))
        return i + 1, unfinished()

    n_done, _ = lax.while_loop(lambda c: jnp.logical_and(c[0] < nfull, c[1]), body,
                               (jnp.int32(0), unfinished()))
    pipe.finish(_Block(nfull - n_done, None, handoff))
    o_ref[0] = jnp.concatenate([acc_ref[h] for h in range(nh)], axis=0).T.astype(o_ref.dtype)


def _prompt_attention(kernel_fn, name, qt, kbf, vt, ct, group):
    b, _, t = qt.shape
    nkb = t // KEY_BLOCK
    wstep = HEADS_PER_STEP * HEAD_DIM
    nstep = WIDTH // wstep
    off = group * nstep
    in_specs = [
        pl.BlockSpec((1, wstep, QUERY_BLOCK), lambda bi, p, j: (bi, off + p, j)),
        pl.BlockSpec((1, t, wstep), lambda bi, p, j: (bi, 0, off + p)),
        pl.BlockSpec((1, nkb, wstep, KEY_BLOCK), lambda bi, p, j: (bi, 0, off + p, 0)),
    ]
    args = [qt, kbf, vt]
    acc_rows = HEAD_DIM
    if ct is not None:
        in_specs.append(pl.BlockSpec((1, HEADS_PER_STEP, nkb, KEY_BLOCK), lambda bi, p, j: (bi, p, 0, 0)))
        args.append(ct)
        acc_rows += 16
    stages, gap = (_FOX_STAGES, _FOX_GAP) if group == 0 else (_SB_STAGES, _SB_GAP)
    scratch = [pltpu.VMEM((HEADS_PER_STEP, 1, QUERY_BLOCK), _f32),
               pltpu.VMEM((HEADS_PER_STEP, acc_rows, QUERY_BLOCK), _f32)]
    if group == 1:
        scratch += [pltpu.VMEM((HEADS_PER_STEP // HEADS_PER_SLAB, 1, LANES), _f32),
                    pltpu.VMEM((HEADS_PER_STEP, 1, QUERY_BLOCK), _f32)]
    scratch += [pltpu.VMEM(*_intermediate_type(name))
                for (name, _) in _handoff_keys(stages, HEADS_PER_STEP, gap)]
    return pl.pallas_call(
        kernel_fn,
        out_shape=jax.ShapeDtypeStruct((b, t, WIDTH), _bf16),
        grid=(b, nstep, t // QUERY_BLOCK),
        in_specs=in_specs,
        out_specs=pl.BlockSpec((1, QUERY_BLOCK, wstep), lambda bi, p, j: (bi, j, p)),
        scratch_shapes=scratch,
        compiler_params=pltpu.CompilerParams(
            dimension_semantics=("arbitrary", "arbitrary", "arbitrary"), vmem_limit_bytes=VMEM_LIMIT),
        name=name,
    )(*args)


def _merge_kernel(oa_ref, ob_ref, z_ref, x_ref, ga_ref, gb_ref, w_ref, fg_ref, out_ref, *, final):
    def gated(o, g, z):
        return (_rmsnorm(o, g) * (z * (1.0 / (1.0 + jnp.exp(-z))))).astype(_bf16)

    z = z_ref[0].astype(_f32)
    ya = gated(oa_ref[0].astype(_f32), ga_ref[...], z[:, 0:WIDTH])
    yb = gated(ob_ref[0].astype(_f32), gb_ref[...], z[:, WIDTH:2 * WIDTH])
    y = (jnp.dot(ya, w_ref[0:WIDTH, :], preferred_element_type=_f32)
         + jnp.dot(yb, w_ref[WIDTH:2 * WIDTH, :], preferred_element_type=_f32))
    xn = x_ref[0] + y
    out_ref[0] = _rmsnorm(xn, fg_ref[...]) if final else xn


def _merge(oa, ob, z, x, ga, gb, w_out, fg, final):
    b, t, d = x.shape
    tm = min(PROJ_ROWS, t)
    row = lambda bi, ti: (bi, ti, 0)
    const2 = lambda bi, ti: (0, 0)
    return pl.pallas_call(
        functools.partial(_merge_kernel, final=final),
        out_shape=jax.ShapeDtypeStruct((b, t, d), _f32),
        grid=(b, t // tm),
        in_specs=[
            pl.BlockSpec((1, tm, WIDTH), row),
            pl.BlockSpec((1, tm, WIDTH), row),
            pl.BlockSpec((1, tm, 2 * WIDTH), row),
            pl.BlockSpec((1, tm, d), row),
            pl.BlockSpec((1, WIDTH), const2),
            pl.BlockSpec((1, WIDTH), const2),
            pl.BlockSpec(w_out.shape, const2),
            pl.BlockSpec((1, d), const2),
        ],
        out_specs=pl.BlockSpec((1, tm, d), row),
        compiler_params=pltpu.CompilerParams(
            dimension_semantics=("arbitrary", "arbitrary"), vmem_limit_bytes=VMEM_LIMIT),
        name="merge_final" if final else "merge",
    )(oa, ob, z, x, ga, gb, w_out, fg)


def _sample_inproj_kernel(x_ref, g_ref, w_ref, wf_ref, bf_ref, u_ref, logf_ref, logft_ref):
    h = _rmsnorm(x_ref[...], g_ref[...]).astype(_bf16)
    u_ref[...] = jnp.dot(h, w_ref[...], preferred_element_type=_f32)
    fa = jnp.dot(h, wf_ref[...], preferred_element_type=_f32) + bf_ref[...]
    logf = _log_sigmoid(fa)
    logf_ref[...] = logf[:, 0:N_FORGET]
    logft_ref[...] = logf.T[0:N_FORGET, :]


def _sample_inproj(x2d, g, w_main, w_f, b_f):
    r, d = x2d.shape
    return pl.pallas_call(
        _sample_inproj_kernel,
        out_shape=(jax.ShapeDtypeStruct((r, w_main.shape[1]), _f32),
                   jax.ShapeDtypeStruct((r, N_FORGET), _f32),
                   jax.ShapeDtypeStruct((N_FORGET, r), _f32)),
        compiler_params=pltpu.CompilerParams(vmem_limit_bytes=VMEM_LIMIT),
        name="sample_inproj",
    )(x2d, g, w_main, w_f, b_f)


def _expand_heads(m, rows_per_head):
    return jnp.concatenate(
        [jnp.broadcast_to(m[h:h + 1, :], (rows_per_head, m.shape[1])) for h in range(N_HEADS)], axis=0)


def _nt_dot(a, b):
    return lax.dot_general(a, b, (((1,), (1,)), ((), ())), preferred_element_type=_f32)


def _head_cols(x, h):
    return x[:, h * HEAD_DIM:(h + 1) * HEAD_DIM]


def _scores_cached(q, kt_ref):
    return jnp.concatenate([jnp.dot(_head_cols(q, h), kt_ref[0, 0, h].astype(_bf16), preferred_element_type=_f32)
                            for h in range(N_HEADS)], axis=0)


def _values_cached(p, vt_ref, tq):
    return jnp.concatenate([_nt_dot(p[h * tq:(h + 1) * tq], vt_ref[0, 0, h].astype(_bf16))
                            for h in range(N_HEADS)], axis=0)


def _scores_new(q, knew):
    return jnp.concatenate([_nt_dot(_head_cols(q, h), _head_cols(knew, h)) for h in range(N_HEADS)], axis=0)


def _values_new(p, vnew, tq):
    return jnp.concatenate([jnp.dot(p[h * tq:(h + 1) * tq], _head_cols(vnew, h), preferred_element_type=_f32)
                            for h in range(N_HEADS)], axis=0)


def _heads_to_lanes(o, tq):
    return jnp.concatenate([o[h * tq:(h + 1) * tq] for h in range(N_HEADS)], axis=1)


def _sample_attn_kernel(u_ref, lfn_ref, lfc_ref, ka_ref, va_ref, kb_ref, vb_ref, tri_ref,
                        oa_ref, ob_ref,
                        cc_ref, xnew_ref, m_ref, l_ref, acca_ref, later_ref, accb_ref, *, tq, nch):
    i = pl.program_id(1)
    kc = SAMPLE_CHUNK
    rows = N_HEADS * tq
    u = u_ref[0]
    qa = (u[:, 0:WIDTH] * SCALE).astype(_bf16)
    qb = (u[:, 4 * WIDTH:5 * WIDTH] * SCALE).astype(_bf16)
    rq = lax.broadcasted_iota(jnp.int32, (rows, LANES), 0) % tq
    ln = lax.broadcasted_iota(jnp.int32, (rows, LANES), 1)
    pad = jnp.zeros((LANES - tq, WIDTH), _f32)

    @pl.when(i == 0)
    def _():
        carry = jnp.zeros((16, 1), _f32)
        zrows = jnp.zeros((16 - N_FORGET, kc), _f32)
        for ch in range(nch):
            x = jnp.concatenate([lfc_ref[0, :, ch * kc:(ch + 1) * kc], zrows], axis=0)
            cch = _lane_cumsum(x, carry)
            cc_ref[ch] = cch
            carry = cch[:, kc - 1:kc]
        xn = jnp.concatenate([lfn_ref[0], jnp.zeros((16 - N_FORGET, LANES), _f32)], axis=0)
        cnew = _lane_cumsum(xn, carry)
        xnew = _expand_heads(cnew[0:N_HEADS, :], tq)
        xnew_ref[...] = xnew
        cq = jnp.sum(jnp.where(ln == rq, xnew, 0.0), axis=1, keepdims=True)

        knew = jnp.concatenate([u[:, WIDTH:2 * WIDTH], pad], axis=0).astype(_bf16)
        vnew = jnp.concatenate([u[:, 2 * WIDTH:3 * WIDTH], pad], axis=0).astype(_bf16)
        logit = jnp.where(ln <= rq, _scores_new(qa, knew) + (cq - xnew), NEG_BIG)
        m = jnp.max(logit, axis=1, keepdims=True)
        p = jnp.exp(logit - m)
        m_ref[...] = m
        l_ref[...] = jnp.sum(p, axis=1, keepdims=True)
        acca_ref[...] = _values_new(p.astype(_bf16), vnew, tq)

        knew = jnp.concatenate([u[:, 5 * WIDTH:6 * WIDTH], pad], axis=0).astype(_bf16)
        vnew = jnp.concatenate([u[:, 6 * WIDTH:7 * WIDTH], pad], axis=0).astype(_bf16)
        z = _scores_new(qb, knew)
        msk = ln < rq
        sp = jnp.where(msk, _softplus(z), 0.0)
        r = lax.broadcasted_iota(jnp.int32, (2 * LANES, LANES), 0)
        c = lax.broadcasted_iota(jnp.int32, (2 * LANES, LANES), 1)
        r = jnp.where(r >= LANES, r - LANES, r)
        suffix = jnp.where(r > c, 1.0, 0.0).astype(_bf16)
        within = jnp.dot(_split2_cols(sp), suffix, preferred_element_type=_f32)
        a = jnp.where(msk, jnp.exp(z - sp - within), 0.0)
        accb_ref[...] = _values_new(a.astype(_bf16), vnew, tq)
        later_ref[...] = jnp.sum(sp, axis=1, keepdims=True)

    xq = xnew_ref[...]
    cq = jnp.sum(jnp.where(ln == rq, xq, 0.0), axis=1, keepdims=True)
    ck = _expand_heads(cc_ref[i][0:N_HEADS, :], tq)
    logit = _scores_cached(qa, ka_ref) + (cq - ck)
    m_old = m_ref[...]
    m_new = jnp.maximum(m_old, jnp.max(logit, axis=1, keepdims=True))
    alpha = jnp.exp(m_old - m_new)
    p = jnp.exp(logit - m_new)
    m_ref[...] = m_new
    l_ref[...] = alpha * l_ref[...] + jnp.sum(p, axis=1, keepdims=True)
    acca_ref[...] = alpha * acca_ref[...] + _values_cached(p.astype(_bf16), va_ref, tq)

    z = _scores_cached(qb, kb_ref)
    sp = _softplus(z)
    within = jnp.dot(_split2_cols(sp), tri_ref[...], preferred_element_type=_f32)
    a = jnp.exp(z - sp - within - later_ref[...])
    accb_ref[...] = accb_ref[...] + _values_cached(a.astype(_bf16), vb_ref, tq)
    later_ref[...] = later_ref[...] + jnp.sum(sp, axis=1, keepdims=True)

    @pl.when(i == nch - 1)
    def _():
        oa_ref[0] = _heads_to_lanes(acca_ref[...] / l_ref[...], tq)
        ob_ref[0] = _heads_to_lanes(accb_ref[...], tq)


def _sample_attention(layer, u, lf_new, lf_cache_t, ck_a, cv_a, ck_b, cv_b, tri):
    nb, tq, _ = u.shape
    past = ck_a.shape[4]
    kc = SAMPLE_CHUNK
    nch = past // kc
    rows = N_HEADS * tq
    chunk = (1, 1, N_HEADS, HEAD_DIM, kc)
    fwd = lambda bi, i: (layer, bi, 0, 0, i)
    rev = lambda bi, i: (layer, bi, 0, 0, nch - 1 - i)
    per_b = lambda bi, i: (bi, 0, 0)
    return pl.pallas_call(
        functools.partial(_sample_attn_kernel, tq=tq, nch=nch),
        out_shape=(jax.ShapeDtypeStruct((nb, tq, WIDTH), _f32),
                   jax.ShapeDtypeStruct((nb, tq, WIDTH), _f32)),
        grid=(nb, nch),
        in_specs=[
            pl.BlockSpec((1, tq, u.shape[2]), per_b),
            pl.BlockSpec((1, N_FORGET, LANES), per_b),
            pl.BlockSpec((1, N_FORGET, past), per_b),
            pl.BlockSpec(chunk, fwd),
            pl.BlockSpec(chunk, fwd),
            pl.BlockSpec(chunk, rev),
            pl.BlockSpec(chunk, rev),
            pl.BlockSpec(tri.shape, lambda bi, i: (0, 0)),
        ],
        out_specs=(pl.BlockSpec((1, tq, WIDTH), per_b), pl.BlockSpec((1, tq, WIDTH), per_b)),
        scratch_shapes=[
            pltpu.VMEM((nch, 16, kc), _f32),
            pltpu.VMEM((rows, LANES), _f32),
            pltpu.VMEM((rows, 1), _f32),
            pltpu.VMEM((rows, 1), _f32),
            pltpu.VMEM((rows, HEAD_DIM), _f32),
            pltpu.VMEM((rows, 1), _f32),
            pltpu.VMEM((rows, HEAD_DIM), _f32),
        ],
        compiler_params=pltpu.CompilerParams(
            dimension_semantics=("arbitrary", "arbitrary"), vmem_limit_bytes=VMEM_LIMIT),
        name="sample_attention",
    )(u, lf_new, lf_cache_t, ck_a, cv_a, ck_b, cv_b, tri)


def _split_weights(w):
    f0 = 3 * WIDTH
    w_main = jnp.concatenate([w[:, 0:f0], w[:, f0 + N_FORGET:]], axis=1).astype(_bf16)
    w_f = jnp.pad(w[:, f0:f0 + N_FORGET], ((0, 0), (0, LANES - N_FORGET))).astype(_bf16)
    return w_main, w_f


def kernel(x_prompt, x_sample, cache_fox_k, cache_fox_v, cache_fox_logf, cache_sb_k, cache_sb_v,
           norm_g, w_in, b_f, out_norm_a, out_norm_b, w_out, final_norm_g):
    depth = norm_g.shape[0]
    b, t, d = x_prompt.shape
    nb, tq, _ = x_sample.shape
    past = cache_fox_k.shape[2]
    nkb = t // KEY_BLOCK
    assert t % QUERY_BLOCK == 0 and t % PROJ_ROWS == 0 and past % SAMPLE_CHUNK == 0
    assert tq <= 16 and (nb * tq) % 8 == 0

    kc = SAMPLE_CHUNK
    r = lax.broadcasted_iota(jnp.int32, (2 * kc, kc), 0) % kc
    c = lax.broadcasted_iota(jnp.int32, (2 * kc, kc), 1)
    tri = jnp.where(r > c, 1.0, 0.0).astype(_bf16)
    fg = final_norm_g.reshape(1, d)
    time_minor = lambda a: a.transpose(0, 1, 3, 4, 2)
    cache_fox_k, cache_fox_v = time_minor(cache_fox_k), time_minor(cache_fox_v)
    cache_sb_k, cache_sb_v = time_minor(cache_sb_k), time_minor(cache_sb_v)

    xp, xs = x_prompt, x_sample.reshape(1, nb * tq, d)
    stacked, sample_new = None, []
    for l in range(depth):
        w_main, w_f = _split_weights(w_in[l])
        g = norm_g[l].reshape(1, d)
        bf = jnp.pad(b_f[l], (0, LANES - N_FORGET)).reshape(1, LANES)
        ga, gb = out_norm_a[l].reshape(1, WIDTH), out_norm_b[l].reshape(1, WIDTH)
        wo = w_out[l].astype(_bf16)
        final = l == depth - 1

        outs = _inproj(l, depth, stacked, xp, g, w_main, w_f, bf)
        stacked, (ct, qt, kbf, vt, z) = outs[:N_STACKED], outs[N_STACKED:]
        ct = ct.reshape(b, N_HEADS, nkb, KEY_BLOCK)
        oa = _prompt_attention(_fox_kernel, "fox_attention", qt, kbf, vt, ct, 0)
        ob = _prompt_attention(_sb_kernel, "sb_attention", qt, kbf, vt, None, 1)
        xp = _merge(oa, ob, z, xp, ga, gb, wo, fg, final)

        u, logf_s, logft_s = _sample_inproj(xs[0], g, w_main, w_f, bf)
        u3 = u.reshape(nb, tq, u.shape[1])
        lf_new = jnp.pad(logft_s.reshape(N_FORGET, nb, tq).transpose(1, 0, 2),
                         ((0, 0), (0, 0), (0, LANES - tq)))
        lf_cache_t = cache_fox_logf[l].transpose(0, 2, 1)
        oa_s, ob_s = _sample_attention(l, u3, lf_new, lf_cache_t,
                                       cache_fox_k, cache_fox_v, cache_sb_k, cache_sb_v, tri)
        z_s = jnp.concatenate([u[:, 3 * WIDTH:4 * WIDTH], u[:, 7 * WIDTH:8 * WIDTH]], axis=1)
        xs = _merge(oa_s.reshape(1, nb * tq, WIDTH), ob_s.reshape(1, nb * tq, WIDTH),
                    z_s.reshape(1, nb * tq, 2 * WIDTH), xs, ga, gb, wo, fg, final)
        hs = lambda a: a.reshape(nb, tq, N_HEADS, HEAD_DIM)
        sample_new.append((hs(u[:, WIDTH:2 * WIDTH]), hs(u[:, 2 * WIDTH:3 * WIDTH]),
                           logf_s.reshape(nb, tq, N_FORGET),
                           hs(u[:, 5 * WIDTH:6 * WIDTH]), hs(u[:, 6 * WIDTH:7 * WIDTH])))

    heads = lambda a: a.reshape(depth, b, N_HEADS, HEAD_DIM, t).transpose(0, 1, 4, 2, 3)
    kat, vat, kbt, vbt, lft = stacked
    stack = lambda items, i: jnp.stack([it[i] for it in items])
    return (xp, xs.reshape(nb, tq, d),
            heads(kat), heads(vat), lft.transpose(0, 1, 3, 2), heads(kbt), heads(vbt),
            *(stack(sample_new, i) for i in range(5)))
```

```python
import functools

import jax
import jax.numpy as jnp
from jax import lax
from jax.experimental import pallas as pl
from jax.experimental.pallas import tpu as pltpu

HEAD_DIM = 64
N_HEADS = 8
WIDTH = N_HEADS * HEAD_DIM
N_FORGET = N_HEADS
EPS = 1e-6
SCALE = HEAD_DIM ** -0.5
LOG2E = 1.4426950408889634
NEG_BIG = -1e30
UNDERFLOW_EXP2 = 150.0

LANES = 128
KEY_BLOCK = 256
QUERY_BLOCK = 256
SLAB = 256
HEADS_PER_SLAB = SLAB // HEAD_DIM
HEADS_PER_STEP = 8
PROJ_ROWS = 256
MERGE_ROWS = 512
SAMPLE_CHUNK = 512
VMEM_LIMIT = 48 * 1024 * 1024

_f32 = jnp.float32
_bf16 = jnp.bfloat16


def _softplus(z):
    neg_abs = lax.bitcast_convert_type(
        lax.bitcast_convert_type(z, jnp.uint32) | jnp.uint32(0x80000000), _f32)
    return jnp.maximum(z, 0.0) + jnp.log(1.0 + jnp.exp(neg_abs))


def _softplus2(zs):
    neg_abs = lax.bitcast_convert_type(
        lax.bitcast_convert_type(zs, jnp.uint32) | jnp.uint32(0x80000000), _f32)
    return jnp.maximum(zs, 0.0) + jnp.log(1.0 + jnp.exp2(neg_abs)) * LOG2E


def _log_sigmoid(x):
    return jnp.minimum(x, 0.0) - jnp.log(1.0 + jnp.exp(-jnp.abs(x)))


def _rmsnorm(x, g):
    return x * lax.rsqrt(jnp.mean(x * x, axis=-1, keepdims=True) + EPS) * g


def _split3(x):
    a1 = x.astype(_bf16)
    r1 = x - a1.astype(_f32)
    a2 = r1.astype(_bf16)
    a3 = (r1 - a2.astype(_f32)).astype(_bf16)
    return a1, a2, a3


def _split2_rows(x):
    hi = lax.bitcast_convert_type(
        lax.bitcast_convert_type(x, jnp.uint32) & jnp.uint32(0xFFFF0000), _f32)
    return jnp.concatenate([hi.astype(_bf16), (x - hi).astype(_bf16)], axis=0)


def _split2_cols(x):
    hi = lax.bitcast_convert_type(
        lax.bitcast_convert_type(x, jnp.uint32) & jnp.uint32(0xFFFF0000), _f32)
    return jnp.concatenate([hi.astype(_bf16), (x - hi).astype(_bf16)], axis=1)


def _lane_cumsum(x, carry):
    n = x.shape[1]
    r = lax.broadcasted_iota(jnp.int32, (n, n), 0)
    c = lax.broadcasted_iota(jnp.int32, (n, n), 1)
    tri = jnp.where(r <= c, 1.0, 0.0).astype(_bf16)
    a1, a2, a3 = _split3(x)
    dot = functools.partial(jnp.dot, preferred_element_type=_f32)
    return carry + (dot(a1, tri) + dot(a2, tri) + dot(a3, tri))


N_STACKED = 5


def _inproj_kernel(x_ref, g_ref, w_ref, wf_ref, bf_ref, *refs, first):
    if not first:
        refs = refs[N_STACKED:]
    kat_ref, vat_ref, kbt_ref, vbt_ref, lft_ref, ct_ref, qt_ref, kbf_ref, vt_ref, z_ref, carry_ref = refs
    tm = x_ref.shape[1]
    h = _rmsnorm(x_ref[0], g_ref[...]).astype(_bf16)

    def seg(i):
        return jnp.dot(h, w_ref[:, i * WIDTH:(i + 1) * WIDTH], preferred_element_type=_f32)

    def put_stacked(ref, value):
        for layer in range(ref.shape[0]):
            ref[layer, 0] = value

    for grp, (kt_ref, vtf_ref) in enumerate(((kat_ref, vat_ref), (kbt_ref, vbt_ref))):
        lo, hi = grp * WIDTH, (grp + 1) * WIDTH
        q = seg(4 * grp + 0) * (SCALE * LOG2E)
        qt_ref[0, lo:hi, :] = q.T.astype(_bf16)
        k = seg(4 * grp + 1)
        put_stacked(kt_ref, k.T)
        kbf_ref[0, :, lo:hi] = k.astype(_bf16)
        vt = seg(4 * grp + 2).T
        put_stacked(vtf_ref, vt)
        for s in range(tm // KEY_BLOCK):
            vt_ref[0, s, lo:hi, :] = vt[:, s * KEY_BLOCK:(s + 1) * KEY_BLOCK].astype(_bf16)
        z_ref[0, :, lo:hi] = seg(4 * grp + 3).astype(z_ref.dtype)

    fa = jnp.dot(h, wf_ref[...], preferred_element_type=_f32) + bf_ref[...]
    logft = _log_sigmoid(fa).T[0:16, :]
    put_stacked(lft_ref, logft[0:N_FORGET, :])

    @pl.when(pl.program_id(1) == 0)
    def _():
        carry_ref[...] = jnp.zeros_like(carry_ref)

    c = _lane_cumsum(logft, carry_ref[:, 0:1])
    ct_ref[0] = c[0:N_FORGET, :] * LOG2E
    carry_ref[...] = jnp.broadcast_to(c[:, tm - 1:tm], carry_ref.shape)


def _inproj(layer, depth, stacked, x, g, w_main, w_f, b_f):
    b, t, d = x.shape
    tm = PROJ_ROWS
    nkb = t // KEY_BLOCK
    first = stacked is None
    row = lambda bi, ti: (bi, ti, 0)
    const2 = lambda bi, ti: (0, 0)
    feat = lambda bi, ti: (bi, 0, ti)
    widths = (WIDTH, WIDTH, WIDTH, WIDTH, N_FORGET)
    slabs = depth if first else 1
    slab = lambda bi, ti: (0 if first else layer, bi, 0, ti)
    out_shape = tuple(jax.ShapeDtypeStruct((depth, b, w, t), _f32) for w in widths) + (
        jax.ShapeDtypeStruct((b, N_FORGET, t), _f32),
        jax.ShapeDtypeStruct((b, 2 * WIDTH, t), _bf16),
        jax.ShapeDtypeStruct((b, t, 2 * WIDTH), _bf16),
        jax.ShapeDtypeStruct((b, nkb, 2 * WIDTH, KEY_BLOCK), _bf16),
        jax.ShapeDtypeStruct((b, t, 2 * WIDTH), _bf16),
    )
    out_specs = tuple(pl.BlockSpec((slabs, 1, w, tm), slab) for w in widths) + (
        pl.BlockSpec((1, N_FORGET, tm), feat),
        pl.BlockSpec((1, 2 * WIDTH, tm), feat),
        pl.BlockSpec((1, tm, 2 * WIDTH), row),
        pl.BlockSpec((1, tm // KEY_BLOCK, 2 * WIDTH, KEY_BLOCK), lambda bi, ti: (bi, ti, 0, 0)),
        pl.BlockSpec((1, tm, 2 * WIDTH), row),
    )
    in_specs = [
        pl.BlockSpec((1, tm, d), row),
        pl.BlockSpec((1, d), const2),
        pl.BlockSpec(w_main.shape, const2),
        pl.BlockSpec(w_f.shape, const2),
        pl.BlockSpec((1, LANES), const2),
    ]
    args = [x, g, w_main, w_f, b_f]
    aliases = {}
    if not first:
        in_specs += [pl.BlockSpec(memory_space=pl.ANY)] * N_STACKED
        aliases = {len(args) + i: i for i in range(N_STACKED)}
        args += list(stacked)
    return pl.pallas_call(
        functools.partial(_inproj_kernel, first=first),
        out_shape=out_shape,
        grid=(b, t // tm),
        in_specs=in_specs,
        out_specs=out_specs,
        scratch_shapes=[pltpu.VMEM((16, LANES), _f32)],
        input_output_aliases=aliases,
        compiler_params=pltpu.CompilerParams(
            dimension_semantics=("arbitrary", "arbitrary"), vmem_limit_bytes=VMEM_LIMIT),
        name="prompt_inproj",
    )(*args)


def _head_queries(qt_ref):
    out = []
    for s in range(qt_ref.shape[1] // SLAB):
        qt = qt_ref[0, s * SLAB:(s + 1) * SLAB, :]
        head = lax.broadcasted_iota(jnp.int32, qt.shape, 0) // HEAD_DIM
        out += [jnp.where(head == hh, qt, jnp.zeros_like(qt)) for hh in range(SLAB // HEAD_DIM)]
    return out


_FOX_STAGES = (("scores", (), ("s",)), ("weights", ("s",), ("p", "alpha")), ("values", ("p", "alpha"), ()))
_SB_STAGES = (("scores", (), ("z",)), ("survival", ("z",), ("sp",)), ("suffix_sums", ("sp",), ("w",)),
              ("weights", ("z", "w"), ("a",)), ("values", ("a",), ()))
_FOX_GAP = 4
_SB_GAP = 1


def _intermediate_type(name):
    tile = (KEY_BLOCK, QUERY_BLOCK)
    return {"s": (tile, _f32), "z": (tile, _f32), "p": (tile, _bf16), "sp": (tile, _bf16), "a": (tile, _bf16),
            "alpha": ((1, QUERY_BLOCK), _f32), "w": ((KEY_BLOCK + 16, QUERY_BLOCK), _f32)}[name]


def _handoff_keys(stages, nh, gap):
    keys = []
    for h in range(nh):
        done = min(len(stages), (nh - 1 - h) // gap + 1)
        made = {n for (_, _, w) in stages[:done] for n in w}
        need = {n for (_, r, _) in stages[done:] for n in r}
        keys += [(n, h) for n in sorted(made & need)]
    return keys


class _Block:
    def __init__(self, kb, mask, handoff):
        self.kb, self.mask, self.handoff, self.vals = kb, mask, handoff, {}

    def put(self, name, h, value):
        self.vals[(name, h)] = value
        if (name, h) in self.handoff:
            self.handoff[(name, h)][...] = value

    def get(self, name, h):
        if (name, h) in self.vals:
            return self.vals[(name, h)]
        return self.handoff[(name, h)][...]


class _HeadPipeline:
    def __init__(self, stages, fns, nh, gap):
        self.fns, self.nh, self.gap = [fns[name] for (name, _, _) in stages], nh, gap
        self.tail = gap * (len(stages) - 1)
        assert self.tail <= nh

    def _step(self, blk, step):
        for k, fn in enumerate(self.fns):
            if 0 <= step - self.gap * k < self.nh:
                fn(blk, step - self.gap * k)

    def advance(self, prev, cur):
        for step in range(self.nh):
            if prev is not None and step < self.tail:
                self._step(prev, self.nh + step)
            self._step(cur, step)

    def finish(self, prev):
        for step in range(self.tail):
            self._step(prev, self.nh + step)


def _key_slab(k_ref, kb, h):
    s = h // HEADS_PER_SLAB
    return k_ref[0, pl.ds(pl.multiple_of(kb * KEY_BLOCK, KEY_BLOCK), KEY_BLOCK), s * SLAB:(s + 1) * SLAB]


def _fox_kernel(qt_ref, k_ref, vt_ref, ct_ref, o_ref, m_ref, acc_ref, *handoff_refs):
    j = pl.program_id(2)
    nh = qt_ref.shape[1] // HEAD_DIM
    handoff = dict(zip(_handoff_keys(_FOX_STAGES, nh, _FOX_GAP), handoff_refs))
    qh = _head_queries(qt_ref)
    ndiag = QUERY_BLOCK // KEY_BLOCK
    nfull = ndiag * j
    cq = [jnp.concatenate([ct_ref[0, h, pl.ds(nfull + d, 1), :] for d in range(ndiag)], axis=1)
          for h in range(nh)]
    kio = lax.broadcasted_iota(jnp.int32, (KEY_BLOCK, QUERY_BLOCK), 0)
    qio = lax.broadcasted_iota(jnp.int32, (KEY_BLOCK, QUERY_BLOCK), 1)
    causal = [d * KEY_BLOCK + kio <= qio for d in range(ndiag)]
    ones_rows = (lax.broadcasted_iota(jnp.int32, (16, KEY_BLOCK), 0) == 0).astype(_bf16)

    m_ref[...] = jnp.full(m_ref.shape, NEG_BIG, _f32)
    acc_ref[...] = jnp.zeros(acc_ref.shape, _f32)

    def scores(blk, h):
        s = jnp.dot(_key_slab(k_ref, blk.kb, h), qh[h], preferred_element_type=_f32)
        blk.put("s", h, s if blk.mask is None else jnp.where(blk.mask, s, NEG_BIG))

    def weights(blk, h):
        ck_row = ct_ref[0, h, pl.ds(blk.kb, 1), :]
        ck = jnp.broadcast_to(ck_row, (LANES, KEY_BLOCK)).T
        ck = jnp.concatenate([ck] * (QUERY_BLOCK // LANES), axis=1)
        logit = blk.get("s", h) + (cq[h] - ck)
        m_old = m_ref[h]
        m_new = jnp.maximum(m_old, jnp.max(logit, axis=0, keepdims=True))
        blk.put("alpha", h, jnp.exp2(m_old - m_new))
        blk.put("p", h, jnp.exp2(logit - m_new).astype(_bf16))
        m_ref[h] = m_new

    def values(blk, h):
        vth = jnp.concatenate([vt_ref[0, blk.kb, h * HEAD_DIM:(h + 1) * HEAD_DIM, :], ones_rows], axis=0)
        acc_ref[h] = (blk.get("alpha", h) * acc_ref[h]
                      + jnp.dot(vth, blk.get("p", h), preferred_element_type=_f32))

    pipe = _HeadPipeline(_FOX_STAGES, {"scores": scores, "weights": weights, "values": values}, nh, _FOX_GAP)
    prev = None
    for d in range(ndiag):
        cur = _Block(nfull + d, causal[d], handoff)
        pipe.advance(prev, cur)
        prev = cur

    last_diag = nfull + ndiag - 1

    @pl.loop(0, nfull)
    def _(i):
        in_flight = jnp.where(i == 0, last_diag, nfull - i)
        pipe.advance(_Block(in_flight, None, handoff), _Block(nfull - 1 - i, None, handoff))

    pipe.finish(_Block(jnp.where(nfull == 0, last_diag, 0), None, handoff))
    o = jnp.concatenate([acc_ref[h, 0:HEAD_DIM] / acc_ref[h, HEAD_DIM:HEAD_DIM + 1] for h in range(nh)], axis=0)
    o_ref[0] = o.T.astype(o_ref.dtype)


def _sb_kernel(qt_ref, k_ref, vt_ref, o_ref, later_ref, acc_ref, kmax_ref, bound_ref, *handoff_refs):
    j = pl.program_id(2)
    nh = qt_ref.shape[1] // HEAD_DIM
    handoff = dict(zip(_handoff_keys(_SB_STAGES, nh, _SB_GAP), handoff_refs))
    qh = _head_queries(qt_ref)
    ndiag = QUERY_BLOCK // KEY_BLOCK
    nfull = ndiag * j
    kio = lax.broadcasted_iota(jnp.int32, (KEY_BLOCK, QUERY_BLOCK), 0)
    qio = lax.broadcasted_iota(jnp.int32, (KEY_BLOCK, QUERY_BLOCK), 1)
    strict = [d * KEY_BLOCK + kio < qio for d in range(ndiag)]
    r = lax.broadcasted_iota(jnp.int32, (KEY_BLOCK + 16, KEY_BLOCK), 0)
    c = lax.broadcasted_iota(jnp.int32, (KEY_BLOCK + 16, KEY_BLOCK), 1)
    suffix = jnp.where((c >= r) | (r == KEY_BLOCK), 1.0, 0.0).astype(_bf16)

    later_ref[...] = jnp.zeros(later_ref.shape, _f32)
    acc_ref[...] = jnp.zeros(acc_ref.shape, _f32)

    def scores(blk, h):
        z = jnp.dot(_key_slab(k_ref, blk.kb, h), qh[h], preferred_element_type=_f32)
        blk.put("z", h, z if blk.mask is None else jnp.where(blk.mask, z, NEG_BIG))

    def survival(blk, h):
        blk.put("sp", h, _softplus2(blk.get("z", h)).astype(_bf16))

    def suffix_sums(blk, h):
        blk.put("w", h, jnp.dot(suffix, blk.get("sp", h), preferred_element_type=_f32))

    def weights(blk, h):
        w = blk.get("w", h)
        later = later_ref[h]
        blk.put("a", h, jnp.exp2(blk.get("z", h) - w[0:KEY_BLOCK] - later).astype(_bf16))
        later_ref[h] = later + w[KEY_BLOCK:KEY_BLOCK + 1]

    def values(blk, h):
        vth = vt_ref[0, blk.kb, h * HEAD_DIM:(h + 1) * HEAD_DIM, :]
        acc_ref[h] = acc_ref[h] + jnp.dot(vth, blk.get("a", h), preferred_element_type=_f32)

    pipe = _HeadPipeline(_SB_STAGES, {"scores": scores, "survival": survival, "suffix_sums": suffix_sums,
                                      "weights": weights, "values": values}, nh, _SB_GAP)
    prev = None
    for d in reversed(range(ndiag)):
        cur = _Block(nfull + d, strict[d], handoff)
        pipe.advance(prev, cur)
        prev = cur

    @pl.when(j == 0)
    def _():
        lane_head = lax.broadcasted_iota(jnp.int32, (SLAB, LANES), 0) // HEAD_DIM
        group = (lane_head == lax.broadcasted_iota(jnp.int32, (SLAB, LANES), 1)).astype(_bf16)
        for s in range(nh // HEADS_PER_SLAB):
            k = k_ref[0, :, s * SLAB:(s + 1) * SLAB].astype(_f32)
            norms2 = jnp.dot((k * k).astype(_bf16), group, preferred_element_type=_f32)
            kmax_ref[s] = jnp.max(norms2, axis=0, keepdims=True)
    lane = lax.broadcasted_iota(jnp.int32, (1, LANES), 1)
    for h in range(nh):
        kmax2 = jnp.max(jnp.where(lane == h % HEADS_PER_SLAB, kmax_ref[h // HEADS_PER_SLAB], 0.0),
                        axis=1, keepdims=True)
        qf = qt_ref[0, h * HEAD_DIM:(h + 1) * HEAD_DIM, :].astype(_f32)
        qn2 = jnp.sum(qf * qf, axis=0, keepdims=True)
        bound_ref[h] = 1.05 * jnp.sqrt(qn2 * kmax2) + UNDERFLOW_EXP2

    weights_stage = [name for (name, _, _) in _SB_STAGES].index("weights")
    late_heads = [h for h in range(nh) if h + _SB_GAP * weights_stage >= nh]
    ones16 = jnp.ones((16, KEY_BLOCK), _bf16)

    def unfinished(newest_kb=None):
        slack, kmean = None, {}
        for h in range(nh):
            later = later_ref[h]
            if newest_kb is not None and h in late_heads:
                s = h // HEADS_PER_SLAB
                if s not in kmean:
                    ksum = jnp.dot(ones16, _key_slab(k_ref, newest_kb, h), preferred_element_type=_f32)
                    kmean[s] = (ksum * (1.0 / KEY_BLOCK)).astype(_bf16)
                zbar = jnp.dot(kmean[s], qh[h], preferred_element_type=_f32)[0:1]
                margin = (bound_ref[h] - UNDERFLOW_EXP2) * 2.0 ** -7
                later = later + (KEY_BLOCK * (1.0 - 2.0 ** -8)) * _softplus2(zbar - margin)
            gap_h = later - bound_ref[h]
            slack = gap_h if slack is None else jnp.minimum(slack, gap_h)
        return jnp.min(slack) < 0.0

    def body(carry):
        i, _ = carry
        pipe.advance(_Block(nfull - i, None, handoff), _Block(nfull - 1 - i, None, handoff))
        return i + 1, unfinished(nfull - 1 - i)

    n_done, _ = lax.while_loop(lambda c: jnp.logical_and(c[0] < nfull, c[1]), body,
                               (jnp.int32(0), unfinished()))
    pipe.finish(_Block(nfull - n_done, None, handoff))
    o_ref[0] = jnp.concatenate([acc_ref[h] for h in range(nh)], axis=0).T.astype(o_ref.dtype)


def _prompt_attention(kernel_fn, name, qt, kbf, vt, ct, group):
    b, _, t = qt.shape
    nkb = t // KEY_BLOCK
    wstep = HEADS_PER_STEP * HEAD_DIM
    nstep = WIDTH // wstep
    off = group * nstep
    in_specs = [
        pl.BlockSpec((1, wstep, QUERY_BLOCK), lambda bi, p, j: (bi, off + p, j)),
        pl.BlockSpec((1, t, wstep), lambda bi, p, j: (bi, 0, off + p)),
        pl.BlockSpec((1, nkb, wstep, KEY_BLOCK), lambda bi, p, j: (bi, 0, off + p, 0)),
    ]
    args = [qt, kbf, vt]
    acc_rows = HEAD_DIM
    if ct is not None:
        in_specs.append(pl.BlockSpec((1, HEADS_PER_STEP, nkb, KEY_BLOCK), lambda bi, p, j: (bi, p, 0, 0)))
        args.append(ct)
        acc_rows += 16
    stages, gap = (_FOX_STAGES, _FOX_GAP) if group == 0 else (_SB_STAGES, _SB_GAP)
    scratch = [pltpu.VMEM((HEADS_PER_STEP, 1, QUERY_BLOCK), _f32),
               pltpu.VMEM((HEADS_PER_STEP, acc_rows, QUERY_BLOCK), _f32)]
    if group == 1:
        scratch += [pltpu.VMEM((HEADS_PER_STEP // HEADS_PER_SLAB, 1, LANES), _f32),
                    pltpu.VMEM((HEADS_PER_STEP, 1, QUERY_BLOCK), _f32)]
    scratch += [pltpu.VMEM(*_intermediate_type(name))
                for (name, _) in _handoff_keys(stages, HEADS_PER_STEP, gap)]
    return pl.pallas_call(
        kernel_fn,
        out_shape=jax.ShapeDtypeStruct((b, t, WIDTH), _bf16),
        grid=(b, nstep, t // QUERY_BLOCK),
        in_specs=in_specs,
        out_specs=pl.BlockSpec((1, QUERY_BLOCK, wstep), lambda bi, p, j: (bi, j, p)),
        scratch_shapes=scratch,
        compiler_params=pltpu.CompilerParams(
            dimension_semantics=("arbitrary", "arbitrary", "arbitrary"), vmem_limit_bytes=VMEM_LIMIT),
        name=name,
    )(*args)


def _merge_kernel(oa_ref, ob_ref, z_ref, x_ref, ga_ref, gb_ref, w_ref, fg_ref, out_ref, *, final):
    def gated(o, g, z):
        return (_rmsnorm(o, g) * (z * (1.0 / (1.0 + jnp.exp(-z))))).astype(_bf16)

    z = z_ref[0].astype(_f32)
    ya = gated(oa_ref[0].astype(_f32), ga_ref[...], z[:, 0:WIDTH])
    yb = gated(ob_ref[0].astype(_f32), gb_ref[...], z[:, WIDTH:2 * WIDTH])
    y = (jnp.dot(ya, w_ref[0:WIDTH, :], preferred_element_type=_f32)
         + jnp.dot(yb, w_ref[WIDTH:2 * WIDTH, :], preferred_element_type=_f32))
    xn = x_ref[0] + y
    out_ref[0] = _rmsnorm(xn, fg_ref[...]) if final else xn


def _merge(oa, ob, z, x, ga, gb, w_out, fg, final):
    b, t, d = x.shape
    tm = min(MERGE_ROWS, t)
    row = lambda bi, ti: (bi, ti, 0)
    const2 = lambda bi, ti: (0, 0)
    return pl.pallas_call(
        functools.partial(_merge_kernel, final=final),
        out_shape=jax.ShapeDtypeStruct((b, t, d), _f32),
        grid=(b, t // tm),
        in_specs=[
            pl.BlockSpec((1, tm, WIDTH), row),
            pl.BlockSpec((1, tm, WIDTH), row),
            pl.BlockSpec((1, tm, 2 * WIDTH), row),
            pl.BlockSpec((1, tm, d), row),
            pl.BlockSpec((1, WIDTH), const2),
            pl.BlockSpec((1, WIDTH), const2),
            pl.BlockSpec(w_out.shape, const2),
            pl.BlockSpec((1, d), const2),
        ],
        out_specs=pl.BlockSpec((1, tm, d), row),
        compiler_params=pltpu.CompilerParams(
            dimension_semantics=("arbitrary", "arbitrary"), vmem_limit_bytes=VMEM_LIMIT),
        name="merge_final" if final else "merge",
    )(oa, ob, z, x, ga, gb, w_out, fg)


def _sample_inproj_kernel(x_ref, g_ref, w_ref, wf_ref, bf_ref, u_ref, logf_ref, logft_ref):
    h = _rmsnorm(x_ref[...], g_ref[...]).astype(_bf16)
    u_ref[...] = jnp.dot(h, w_ref[...], preferred_element_type=_f32)
    fa = jnp.dot(h, wf_ref[...], preferred_element_type=_f32) + bf_ref[...]
    logf = _log_sigmoid(fa)
    logf_ref[...] = logf[:, 0:N_FORGET]
    logft_ref[...] = logf.T[0:N_FORGET, :]


def _sample_inproj(x2d, g, w_main, w_f, b_f):
    r, d = x2d.shape
    return pl.pallas_call(
        _sample_inproj_kernel,
        out_shape=(jax.ShapeDtypeStruct((r, w_main.shape[1]), _f32),
                   jax.ShapeDtypeStruct((r, N_FORGET), _f32),
                   jax.ShapeDtypeStruct((N_FORGET, r), _f32)),
        compiler_params=pltpu.CompilerParams(vmem_limit_bytes=VMEM_LIMIT),
        name="sample_inproj",
    )(x2d, g, w_main, w_f, b_f)


def _expand_heads(m, rows_per_head):
    return jnp.concatenate(
        [jnp.broadcast_to(m[h:h + 1, :], (rows_per_head, m.shape[1])) for h in range(N_HEADS)], axis=0)


def _nt_dot(a, b):
    return lax.dot_general(a, b, (((1,), (1,)), ((), ())), preferred_element_type=_f32)


def _head_cols(x, h):
    return x[:, h * HEAD_DIM:(h + 1) * HEAD_DIM]


def _scores_cached(q, kt_ref):
    return jnp.concatenate([jnp.dot(_head_cols(q, h), kt_ref[0, 0, h].astype(_bf16), preferred_element_type=_f32)
                            for h in range(N_HEADS)], axis=0)


def _values_cached(p, vt_ref, tq):
    return jnp.concatenate([_nt_dot(p[h * tq:(h + 1) * tq], vt_ref[0, 0, h].astype(_bf16))
                            for h in range(N_HEADS)], axis=0)


def _scores_new(q, knew):
    return jnp.concatenate([_nt_dot(_head_cols(q, h), _head_cols(knew, h)) for h in range(N_HEADS)], axis=0)


def _values_new(p, vnew, tq):
    return jnp.concatenate([jnp.dot(p[h * tq:(h + 1) * tq], _head_cols(vnew, h), preferred_element_type=_f32)
                            for h in range(N_HEADS)], axis=0)


def _heads_to_lanes(o, tq):
    return jnp.concatenate([o[h * tq:(h + 1) * tq] for h in range(N_HEADS)], axis=1)


def _sample_attn_kernel(u_ref, lfn_ref, lfc_ref, ka_ref, va_ref, kb_ref, vb_ref, tri_ref,
                        oa_ref, ob_ref,
                        cc_ref, xnew_ref, m_ref, l_ref, acca_ref, later_ref, accb_ref, *, tq, nch):
    i = pl.program_id(1)
    kc = SAMPLE_CHUNK
    rows = N_HEADS * tq
    u = u_ref[0]
    qa = (u[:, 0:WIDTH] * SCALE).astype(_bf16)
    qb = (u[:, 4 * WIDTH:5 * WIDTH] * SCALE).astype(_bf16)
    rq = lax.broadcasted_iota(jnp.int32, (rows, LANES), 0) % tq
    ln = lax.broadcasted_iota(jnp.int32, (rows, LANES), 1)
    pad = jnp.zeros((LANES - tq, WIDTH), _f32)

    @pl.when(i == 0)
    def _():
        carry = jnp.zeros((16, 1), _f32)
        zrows = jnp.zeros((16 - N_FORGET, kc), _f32)
        for ch in range(nch):
            x = jnp.concatenate([lfc_ref[0, :, ch * kc:(ch + 1) * kc], zrows], axis=0)
            cch = _lane_cumsum(x, carry)
            cc_ref[ch] = cch
            carry = cch[:, kc - 1:kc]
        xn = jnp.concatenate([lfn_ref[0], jnp.zeros((16 - N_FORGET, LANES), _f32)], axis=0)
        cnew = _lane_cumsum(xn, carry)
        xnew = _expand_heads(cnew[0:N_HEADS, :], tq)
        xnew_ref[...] = xnew
        cq = jnp.sum(jnp.where(ln == rq, xnew, 0.0), axis=1, keepdims=True)

        knew = jnp.concatenate([u[:, WIDTH:2 * WIDTH], pad], axis=0).astype(_bf16)
        vnew = jnp.concatenate([u[:, 2 * WIDTH:3 * WIDTH], pad], axis=0).astype(_bf16)
        logit = jnp.where(ln <= rq, _scores_new(qa, knew) + (cq - xnew), NEG_BIG)
        m = jnp.max(logit, axis=1, keepdims=True)
        p = jnp.exp(logit - m)
        m_ref[...] = m
        l_ref[...] = jnp.sum(p, axis=1, keepdims=True)
        acca_ref[...] = _values_new(p.astype(_bf16), vnew, tq)

        knew = jnp.concatenate([u[:, 5 * WIDTH:6 * WIDTH], pad], axis=0).astype(_bf16)
        vnew = jnp.concatenate([u[:, 6 * WIDTH:7 * WIDTH], pad], axis=0).astype(_bf16)
        z = _scores_new(qb, knew)
        msk = ln < rq
        sp = jnp.where(msk, _softplus(z), 0.0)
        r = lax.broadcasted_iota(jnp.int32, (2 * LANES, LANES), 0)
        c = lax.broadcasted_iota(jnp.int32, (2 * LANES, LANES), 1)
        r = jnp.where(r >= LANES, r - LANES, r)
        suffix = jnp.where(r > c, 1.0, 0.0).astype(_bf16)
        within = jnp.dot(_split2_cols(sp), suffix, preferred_element_type=_f32)
        a = jnp.where(msk, jnp.exp(z - sp - within), 0.0)
        accb_ref[...] = _values_new(a.astype(_bf16), vnew, tq)
        later_ref[...] = jnp.sum(sp, axis=1, keepdims=True)

    xq = xnew_ref[...]
    cq = jnp.sum(jnp.where(ln == rq, xq, 0.0), axis=1, keepdims=True)
    ck = _expand_heads(cc_ref[i][0:N_HEADS, :], tq)
    logit = _scores_cached(qa, ka_ref) + (cq - ck)
    m_old = m_ref[...]
    m_new = jnp.maximum(m_old, jnp.max(logit, axis=1, keepdims=True))
    alpha = jnp.exp(m_old - m_new)
    p = jnp.exp(logit - m_new)
    m_ref[...] = m_new
    l_ref[...] = alpha * l_ref[...] + jnp.sum(p, axis=1, keepdims=True)
    acca_ref[...] = alpha * acca_ref[...] + _values_cached(p.astype(_bf16), va_ref, tq)

    z = _scores_cached(qb, kb_ref)
    sp = _softplus(z)
    within = jnp.dot(_split2_cols(sp), tri_ref[...], preferred_element_type=_f32)
    a = jnp.exp(z - sp - within - later_ref[...])
    accb_ref[...] = accb_ref[...] + _values_cached(a.astype(_bf16), vb_ref, tq)
    later_ref[...] = later_ref[...] + jnp.sum(sp, axis=1, keepdims=True)

    @pl.when(i == nch - 1)
    def _():
        oa_ref[0] = _heads_to_lanes(acca_ref[...] / l_ref[...], tq)
        ob_ref[0] = _heads_to_lanes(accb_ref[...], tq)


def _sample_attention(layer, u, lf_new, lf_cache_t, ck_a, cv_a, ck_b, cv_b, tri):
    nb, tq, _ = u.shape
    past = ck_a.shape[4]
    kc = SAMPLE_CHUNK
    nch = past // kc
    rows = N_HEADS * tq
    chunk = (1, 1, N_HEADS, HEAD_DIM, kc)
    fwd = lambda bi, i: (layer, bi, 0, 0, i)
    rev = lambda bi, i: (layer, bi, 0, 0, nch - 1 - i)
    per_b = lambda bi, i: (bi, 0, 0)
    return pl.pallas_call(
        functools.partial(_sample_attn_kernel, tq=tq, nch=nch),
        out_shape=(jax.ShapeDtypeStruct((nb, tq, WIDTH), _f32),
                   jax.ShapeDtypeStruct((nb, tq, WIDTH), _f32)),
        grid=(nb, nch),
        in_specs=[
            pl.BlockSpec((1, tq, u.shape[2]), per_b),
            pl.BlockSpec((1, N_FORGET, LANES), per_b),
            pl.BlockSpec((1, N_FORGET, past), per_b),
            pl.BlockSpec(chunk, fwd),
            pl.BlockSpec(chunk, fwd),
            pl.BlockSpec(chunk, rev),
            pl.BlockSpec(chunk, rev),
            pl.BlockSpec(tri.shape, lambda bi, i: (0, 0)),
        ],
        out_specs=(pl.BlockSpec((1, tq, WIDTH), per_b), pl.BlockSpec((1, tq, WIDTH), per_b)),
        scratch_shapes=[
            pltpu.VMEM((nch, 16, kc), _f32),
            pltpu.VMEM((rows, LANES), _f32),
            pltpu.VMEM((rows, 1), _f32),
            pltpu.VMEM((rows, 1), _f32),
            pltpu.VMEM((rows, HEAD_DIM), _f32),
            pltpu.VMEM((rows, 1), _f32),
            pltpu.VMEM((rows, HEAD_DIM), _f32),
        ],
        compiler_params=pltpu.CompilerParams(
            dimension_semantics=("arbitrary", "arbitrary"), vmem_limit_bytes=VMEM_LIMIT),
        name="sample_attention",
    )(u, lf_new, lf_cache_t, ck_a, cv_a, ck_b, cv_b, tri)


def _split_weights(w):
    f0 = 3 * WIDTH
    w_main = jnp.concatenate([w[:, 0:f0], w[:, f0 + N_FORGET:]], axis=1).astype(_bf16)
    w_f = jnp.pad(w[:, f0:f0 + N_FORGET], ((0, 0), (0, LANES - N_FORGET))).astype(_bf16)
    return w_main, w_f


def kernel(x_prompt, x_sample, cache_fox_k, cache_fox_v, cache_fox_logf, cache_sb_k, cache_sb_v,
           norm_g, w_in, b_f, out_norm_a, out_norm_b, w_out, final_norm_g):
    depth = norm_g.shape[0]
    b, t, d = x_prompt.shape
    nb, tq, _ = x_sample.shape
    past = cache_fox_k.shape[2]
    nkb = t // KEY_BLOCK
    assert t % QUERY_BLOCK == 0 and t % PROJ_ROWS == 0 and past % SAMPLE_CHUNK == 0
    assert tq <= 16 and (nb * tq) % 8 == 0

    kc = SAMPLE_CHUNK
    r = lax.broadcasted_iota(jnp.int32, (2 * kc, kc), 0) % kc
    c = lax.broadcasted_iota(jnp.int32, (2 * kc, kc), 1)
    tri = jnp.where(r > c, 1.0, 0.0).astype(_bf16)
    fg = final_norm_g.reshape(1, d)
    time_minor = lambda a: a.transpose(0, 1, 3, 4, 2)
    cache_fox_k, cache_fox_v = time_minor(cache_fox_k), time_minor(cache_fox_v)
    cache_sb_k, cache_sb_v = time_minor(cache_sb_k), time_minor(cache_sb_v)

    xp, xs = x_prompt, x_sample.reshape(1, nb * tq, d)
    stacked, sample_new = None, []
    for l in range(depth):
        w_main, w_f = _split_weights(w_in[l])
        g = norm_g[l].reshape(1, d)
        bf = jnp.pad(b_f[l], (0, LANES - N_FORGET)).reshape(1, LANES)
        ga, gb = out_norm_a[l].reshape(1, WIDTH), out_norm_b[l].reshape(1, WIDTH)
        wo = w_out[l].astype(_bf16)
        final = l == depth - 1

        outs = _inproj(l, depth, stacked, xp, g, w_main, w_f, bf)
        stacked, (ct, qt, kbf, vt, z) = outs[:N_STACKED], outs[N_STACKED:]
        ct = ct.reshape(b, N_HEADS, nkb, KEY_BLOCK)
        oa = _prompt_attention(_fox_kernel, "fox_attention", qt, kbf, vt, ct, 0)
        ob = _prompt_attention(_sb_kernel, "sb_attention", qt, kbf, vt, None, 1)
        xp = _merge(oa, ob, z, xp, ga, gb, wo, fg, final)

        u, logf_s, logft_s = _sample_inproj(xs[0], g, w_main, w_f, bf)
        u3 = u.reshape(nb, tq, u.shape[1])
        lf_new = jnp.pad(logft_s.reshape(N_FORGET, nb, tq).transpose(1, 0, 2),
                         ((0, 0), (0, 0), (0, LANES - tq)))
        lf_cache_t = cache_fox_logf[l].transpose(0, 2, 1)
        oa_s, ob_s = _sample_attention(l, u3, lf_new, lf_cache_t,
                                       cache_fox_k, cache_fox_v, cache_sb_k, cache_sb_v, tri)
        z_s = jnp.concatenate([u[:, 3 * WIDTH:4 * WIDTH], u[:, 7 * WIDTH:8 * WIDTH]], axis=1)
        xs = _merge(oa_s.reshape(1, nb * tq, WIDTH), ob_s.reshape(1, nb * tq, WIDTH),
                    z_s.reshape(1, nb * tq, 2 * WIDTH), xs, ga, gb, wo, fg, final)
        hs = lambda a: a.reshape(nb, tq, N_HEADS, HEAD_DIM)
        sample_new.append((hs(u[:, WIDTH:2 * WIDTH]), hs(u[:, 2 * WIDTH:3 * WIDTH]),
                           logf_s.reshape(nb, tq, N_FORGET),
                           hs(u[:, 5 * WIDTH:6 * WIDTH]), hs(u[:, 6 * WIDTH:7 * WIDTH])))

    heads = lambda a: a.reshape(depth, b, N_HEADS, HEAD_DIM, t).transpose(0, 1, 4, 2, 3)
    kat, vat, kbt, vbt, lft = stacked
    stack = lambda items, i: jnp.stack([it[i] for it in items])
    return (xp, xs.reshape(nb, tq, d),
            heads(kat), heads(vat), lft.transpose(0, 1, 3, 2), heads(kbt), heads(vbt),
            *(stack(sample_new, i) for i in range(5)))
```

```python
import functools

import jax
import jax.numpy as jnp
from jax import lax
from jax.experimental import pallas as pl
from jax.experimental.pallas import tpu as pltpu

HEAD_DIM = 64
N_HEADS = 8
WIDTH = N_HEADS * HEAD_DIM
N_FORGET = N_HEADS
EPS = 1e-6
SCALE = HEAD_DIM ** -0.5
LOG2E = 1.4426950408889634
NEG_BIG = -1e30
UNDERFLOW_EXP2 = 150.0

LANES = 128
KEY_BLOCK = 256
QUERY_BLOCK = 256
SLAB = 256
HEADS_PER_SLAB = SLAB // HEAD_DIM
HEADS_PER_STEP = 8
PROJ_ROWS = 256
MERGE_ROWS = 512
SAMPLE_CHUNK = 512
VMEM_LIMIT = 48 * 1024 * 1024

_f32 = jnp.float32
_bf16 = jnp.bfloat16


def _softplus(z):
    neg_abs = lax.bitcast_convert_type(
        lax.bitcast_convert_type(z, jnp.uint32) | jnp.uint32(0x80000000), _f32)
    return jnp.maximum(z, 0.0) + jnp.log(1.0 + jnp.exp(neg_abs))


def _softplus2(zs):
    neg_abs = lax.bitcast_convert_type(
        lax.bitcast_convert_type(zs, jnp.uint32) | jnp.uint32(0x80000000), _f32)
    return jnp.maximum(zs, 0.0) + jnp.log(1.0 + jnp.exp2(neg_abs)) * LOG2E


def _log_sigmoid(x):
    return jnp.minimum(x, 0.0) - jnp.log(1.0 + jnp.exp(-jnp.abs(x)))


def _rmsnorm(x, g):
    return x * lax.rsqrt(jnp.mean(x * x, axis=-1, keepdims=True) + EPS) * g


def _split3(x):
    a1 = x.astype(_bf16)
    r1 = x - a1.astype(_f32)
    a2 = r1.astype(_bf16)
    a3 = (r1 - a2.astype(_f32)).astype(_bf16)
    return a1, a2, a3


def _split2_rows(x):
    hi = lax.bitcast_convert_type(
        lax.bitcast_convert_type(x, jnp.uint32) & jnp.uint32(0xFFFF0000), _f32)
    return jnp.concatenate([hi.astype(_bf16), (x - hi).astype(_bf16)], axis=0)


def _split2_cols(x):
    hi = lax.bitcast_convert_type(
        lax.bitcast_convert_type(x, jnp.uint32) & jnp.uint32(0xFFFF0000), _f32)
    return jnp.concatenate([hi.astype(_bf16), (x - hi).astype(_bf16)], axis=1)


def _lane_cumsum(x, carry):
    n = x.shape[1]
    r = lax.broadcasted_iota(jnp.int32, (n, n), 0)
    c = lax.broadcasted_iota(jnp.int32, (n, n), 1)
    tri = jnp.where(r <= c, 1.0, 0.0).astype(_bf16)
    a1, a2, a3 = _split3(x)
    dot = functools.partial(jnp.dot, preferred_element_type=_f32)
    return carry + (dot(a1, tri) + dot(a2, tri) + dot(a3, tri))


N_STACKED = 5


def _inproj_kernel(x_ref, g_ref, w_ref, wf_ref, bf_ref, *refs, first):
    if not first:
        refs = refs[N_STACKED:]
    kat_ref, vat_ref, kbt_ref, vbt_ref, lft_ref, ct_ref, qt_ref, kbf_ref, vt_ref, z_ref, carry_ref = refs
    tm = x_ref.shape[1]
    h = _rmsnorm(x_ref[0], g_ref[...]).astype(_bf16)

    def seg(i):
        return jnp.dot(h, w_ref[:, i * WIDTH:(i + 1) * WIDTH], preferred_element_type=_f32)

    def put_stacked(ref, value):
        for layer in range(ref.shape[0]):
            ref[layer, 0] = value

    for grp, (kt_ref, vtf_ref) in enumerate(((kat_ref, vat_ref), (kbt_ref, vbt_ref))):
        lo, hi = grp * WIDTH, (grp + 1) * WIDTH
        q = seg(4 * grp + 0) * (SCALE * LOG2E)
        qt_ref[0, lo:hi, :] = q.T.astype(_bf16)
        k = seg(4 * grp + 1)
        put_stacked(kt_ref, k.T)
        kbf_ref[0, :, lo:hi] = k.astype(_bf16)
        vt = seg(4 * grp + 2).T
        put_stacked(vtf_ref, vt)
        for s in range(tm // KEY_BLOCK):
            vt_ref[0, s, lo:hi, :] = vt[:, s * KEY_BLOCK:(s + 1) * KEY_BLOCK].astype(_bf16)
        z_ref[0, :, lo:hi] = seg(4 * grp + 3).astype(z_ref.dtype)

    fa = jnp.dot(h, wf_ref[...], preferred_element_type=_f32) + bf_ref[...]
    logft = _log_sigmoid(fa).T[0:16, :]
    put_stacked(lft_ref, logft[0:N_FORGET, :])

    @pl.when(pl.program_id(1) == 0)
    def _():
        carry_ref[...] = jnp.zeros_like(carry_ref)

    c = _lane_cumsum(logft, carry_ref[:, 0:1])
    ct_ref[0] = c[0:N_FORGET, :] * LOG2E
    carry_ref[...] = jnp.broadcast_to(c[:, tm - 1:tm], carry_ref.shape)


def _inproj(layer, depth, stacked, x, g, w_main, w_f, b_f):
    b, t, d = x.shape
    tm = PROJ_ROWS
    nkb = t // KEY_BLOCK
    first = stacked is None
    row = lambda bi, ti: (bi, ti, 0)
    const2 = lambda bi, ti: (0, 0)
    feat = lambda bi, ti: (bi, 0, ti)
    widths = (WIDTH, WIDTH, WIDTH, WIDTH, N_FORGET)
    slabs = depth if first else 1
    slab = lambda bi, ti: (0 if first else layer, bi, 0, ti)
    out_shape = tuple(jax.ShapeDtypeStruct((depth, b, w, t), _f32) for w in widths) + (
        jax.ShapeDtypeStruct((b, N_FORGET, t), _f32),
        jax.ShapeDtypeStruct((b, 2 * WIDTH, t), _bf16),
        jax.ShapeDtypeStruct((b, t, 2 * WIDTH), _bf16),
        jax.ShapeDtypeStruct((b, nkb, 2 * WIDTH, KEY_BLOCK), _bf16),
        jax.ShapeDtypeStruct((b, t, 2 * WIDTH), _bf16),
    )
    out_specs = tuple(pl.BlockSpec((slabs, 1, w, tm), slab) for w in widths) + (
        pl.BlockSpec((1, N_FORGET, tm), feat),
        pl.BlockSpec((1, 2 * WIDTH, tm), feat),
        pl.BlockSpec((1, tm, 2 * WIDTH), row),
        pl.BlockSpec((1, tm // KEY_BLOCK, 2 * WIDTH, KEY_BLOCK), lambda bi, ti: (bi, ti, 0, 0)),
        pl.BlockSpec((1, tm, 2 * WIDTH), row),
    )
    in_specs = [
        pl.BlockSpec((1, tm, d), row),
        pl.BlockSpec((1, d), const2),
        pl.BlockSpec(w_main.shape, const2),
        pl.BlockSpec(w_f.shape, const2),
        pl.BlockSpec((1, LANES), const2),
    ]
    args = [x, g, w_main, w_f, b_f]
    aliases = {}
    if not first:
        in_specs += [pl.BlockSpec(memory_space=pl.ANY)] * N_STACKED
        aliases = {len(args) + i: i for i in range(N_STACKED)}
        args += list(stacked)
    return pl.pallas_call(
        functools.partial(_inproj_kernel, first=first),
        out_shape=out_shape,
        grid=(b, t // tm),
        in_specs=in_specs,
        out_specs=out_specs,
        scratch_shapes=[pltpu.VMEM((16, LANES), _f32)],
        input_output_aliases=aliases,
        compiler_params=pltpu.CompilerParams(
            dimension_semantics=("arbitrary", "arbitrary"), vmem_limit_bytes=VMEM_LIMIT),
        name="prompt_inproj",
    )(*args)


def _head_queries(qt_ref):
    out = []
    for s in range(qt_ref.shape[1] // SLAB):
        qt = qt_ref[0, s * SLAB:(s + 1) * SLAB, :]
        head = lax.broadcasted_iota(jnp.int32, qt.shape, 0) // HEAD_DIM
        out += [jnp.where(head == hh, qt, jnp.zeros_like(qt)) for hh in range(SLAB // HEAD_DIM)]
    return out


_FOX_STAGES = (("scores", (), ("s",)), ("logits", ("s",), ("logit", "mnew", "alpha")),
               ("weights", ("logit", "mnew"), ("p",)), ("values", ("p", "alpha"), ()))
_SB_STAGES = (("scores", (), ("z",)), ("survival", ("z",), ("sp",)), ("suffix_sums", ("sp",), ("w",)),
              ("weights", ("z", "w"), ("a",)), ("values", ("a",), ()))
_FOX_GAP = 2
_SB_GAP = 1


def _intermediate_type(name):
    tile = (KEY_BLOCK, QUERY_BLOCK)
    row = (1, QUERY_BLOCK)
    return {"s": (tile, _f32), "z": (tile, _f32), "logit": (tile, _f32), "p": (tile, _bf16), "sp": (tile, _bf16),
            "a": (tile, _bf16), "alpha": (row, _f32), "mnew": (row, _f32),
            "w": ((KEY_BLOCK + 16, QUERY_BLOCK), _f32)}[name]


def _handoff_keys(stages, nh, gap):
    keys = []
    for h in range(nh):
        done = min(len(stages), (nh - 1 - h) // gap + 1)
        made = {n for (_, _, w) in stages[:done] for n in w}
        need = {n for (_, r, _) in stages[done:] for n in r}
        keys += [(n, h) for n in sorted(made & need)]
    return keys


class _Block:
    def __init__(self, kb, mask, handoff):
        self.kb, self.mask, self.handoff, self.vals = kb, mask, handoff, {}

    def put(self, name, h, value):
        self.vals[(name, h)] = value
        if (name, h) in self.handoff:
            self.handoff[(name, h)][...] = value

    def get(self, name, h):
        if (name, h) in self.vals:
            return self.vals[(name, h)]
        return self.handoff[(name, h)][...]


class _HeadPipeline:
    def __init__(self, stages, fns, nh, gap):
        self.fns, self.nh, self.gap = [fns[name] for (name, _, _) in stages], nh, gap
        self.tail = gap * (len(stages) - 1)
        assert self.tail <= nh

    def _step(self, blk, step):
        for k, fn in enumerate(self.fns):
            if 0 <= step - self.gap * k < self.nh:
                fn(blk, step - self.gap * k)

    def advance(self, prev, cur):
        for step in range(self.nh):
            if prev is not None and step < self.tail:
                self._step(prev, self.nh + step)
            self._step(cur, step)

    def finish(self, prev):
        for step in range(self.tail):
            self._step(prev, self.nh + step)


def _key_slab(k_ref, kb, h):
    s = h // HEADS_PER_SLAB
    return k_ref[0, pl.ds(pl.multiple_of(kb * KEY_BLOCK, KEY_BLOCK), KEY_BLOCK), s * SLAB:(s + 1) * SLAB]


def _fox_kernel(qt_ref, k_ref, vt_ref, ct_ref, o_ref, m_ref, acc_ref, *handoff_refs):
    j = pl.program_id(2)
    nh = qt_ref.shape[1] // HEAD_DIM
    handoff = dict(zip(_handoff_keys(_FOX_STAGES, nh, _FOX_GAP), handoff_refs))
    qh = _head_queries(qt_ref)
    ndiag = QUERY_BLOCK // KEY_BLOCK
    nfull = ndiag * j
    c0 = [ct_ref[0, h, pl.ds(nfull, 1), :][:, 0:1] for h in range(nh)]
    kio = lax.broadcasted_iota(jnp.int32, (KEY_BLOCK, QUERY_BLOCK), 0)
    qio = lax.broadcasted_iota(jnp.int32, (KEY_BLOCK, QUERY_BLOCK), 1)
    causal = [d * KEY_BLOCK + kio <= qio for d in range(ndiag)]
    ones_rows = (lax.broadcasted_iota(jnp.int32, (16, KEY_BLOCK), 0) == 0).astype(_bf16)

    m_ref[...] = jnp.full(m_ref.shape, NEG_BIG, _f32)
    acc_ref[...] = jnp.zeros(acc_ref.shape, _f32)

    def scores(blk, h):
        s = jnp.dot(_key_slab(k_ref, blk.kb, h), qh[h], preferred_element_type=_f32)
        blk.put("s", h, s if blk.mask is None else jnp.where(blk.mask, s, NEG_BIG))

    def logits(blk, h):
        ck_row = ct_ref[0, h, pl.ds(blk.kb, 1), :] - c0[h]
        ck = jnp.broadcast_to(ck_row, (LANES, KEY_BLOCK)).T
        ck = jnp.concatenate([ck] * (QUERY_BLOCK // LANES), axis=1)
        logit = blk.get("s", h) - ck
        m_old = m_ref[h]
        m_new = jnp.maximum(m_old, jnp.max(logit, axis=0, keepdims=True))
        blk.put("logit", h, logit)
        blk.put("mnew", h, m_new)
        blk.put("alpha", h, jnp.exp2(m_old - m_new))
        m_ref[h] = m_new

    def weights(blk, h):
        blk.put("p", h, jnp.exp2(blk.get("logit", h) - blk.get("mnew", h)).astype(_bf16))

    def values(blk, h):
        vth = jnp.concatenate([vt_ref[0, blk.kb, h * HEAD_DIM:(h + 1) * HEAD_DIM, :], ones_rows], axis=0)
        acc_ref[h] = (blk.get("alpha", h) * acc_ref[h]
                      + jnp.dot(vth, blk.get("p", h), preferred_element_type=_f32))

    pipe = _HeadPipeline(_FOX_STAGES, {"scores": scores, "logits": logits, "weights": weights, "values": values},
                         nh, _FOX_GAP)
    prev = None
    for d in range(ndiag):
        cur = _Block(nfull + d, causal[d], handoff)
        pipe.advance(prev, cur)
        prev = cur

    last_diag = nfull + ndiag - 1

    @pl.loop(0, nfull)
    def _(i):
        in_flight = jnp.where(i == 0, last_diag, nfull - i)
        pipe.advance(_Block(in_flight, None, handoff), _Block(nfull - 1 - i, None, handoff))

    pipe.finish(_Block(jnp.where(nfull == 0, last_diag, 0), None, handoff))
    o = jnp.concatenate([acc_ref[h, 0:HEAD_DIM] / acc_ref[h, HEAD_DIM:HEAD_DIM + 1] for h in range(nh)], axis=0)
    o_ref[0] = o.T.astype(o_ref.dtype)


def _sb_kernel(qt_ref, k_ref, vt_ref, o_ref, later_ref, acc_ref, kmax_ref, bound_ref, *handoff_refs):
    j = pl.program_id(2)
    nh = qt_ref.shape[1] // HEAD_DIM
    handoff = dict(zip(_handoff_keys(_SB_STAGES, nh, _SB_GAP), handoff_refs))
    qh = _head_queries(qt_ref)
    ndiag = QUERY_BLOCK // KEY_BLOCK
    nfull = ndiag * j
    kio = lax.broadcasted_iota(jnp.int32, (KEY_BLOCK, QUERY_BLOCK), 0)
    qio = lax.broadcasted_iota(jnp.int32, (KEY_BLOCK, QUERY_BLOCK), 1)
    strict = [d * KEY_BLOCK + kio < qio for d in range(ndiag)]
    r = lax.broadcasted_iota(jnp.int32, (KEY_BLOCK + 16, KEY_BLOCK), 0)
    c = lax.broadcasted_iota(jnp.int32, (KEY_BLOCK + 16, KEY_BLOCK), 1)
    suffix = jnp.where((c >= r) | (r == KEY_BLOCK), 1.0, 0.0).astype(_bf16)

    later_ref[...] = jnp.zeros(later_ref.shape, _f32)
    acc_ref[...] = jnp.zeros(acc_ref.shape, _f32)

    def scores(blk, h):
        z = jnp.dot(_key_slab(k_ref, blk.kb, h), qh[h], preferred_element_type=_f32)
        blk.put("z", h, z if blk.mask is None else jnp.where(blk.mask, z, NEG_BIG))

    def survival(blk, h):
        blk.put("sp", h, _softplus2(blk.get("z", h)).astype(_bf16))

    def suffix_sums(blk, h):
        blk.put("w", h, jnp.dot(suffix, blk.get("sp", h), preferred_element_type=_f32))

    def weights(blk, h):
        w = blk.get("w", h)
        later = later_ref[h]
        blk.put("a", h, jnp.exp2(blk.get("z", h) - w[0:KEY_BLOCK] - later).astype(_bf16))
        later_ref[h] = later + w[KEY_BLOCK:KEY_BLOCK + 1]

    def values(blk, h):
        vth = vt_ref[0, blk.kb, h * HEAD_DIM:(h + 1) * HEAD_DIM, :]
        acc_ref[h] = acc_ref[h] + jnp.dot(vth, blk.get("a", h), preferred_element_type=_f32)

    pipe = _HeadPipeline(_SB_STAGES, {"scores": scores, "survival": survival, "suffix_sums": suffix_sums,
                                      "weights": weights, "values": values}, nh, _SB_GAP)
    prev = None
    for d in reversed(range(ndiag)):
        cur = _Block(nfull + d, strict[d], handoff)
        pipe.advance(prev, cur)
        prev = cur

    @pl.when(j == 0)
    def _():
        lane_head = lax.broadcasted_iota(jnp.int32, (SLAB, LANES), 0) // HEAD_DIM
        group = (lane_head == lax.broadcasted_iota(jnp.int32, (SLAB, LANES), 1)).astype(_bf16)
        for s in range(nh // HEADS_PER_SLAB):
            k = k_ref[0, :, s * SLAB:(s + 1) * SLAB].astype(_f32)
            norms2 = jnp.dot((k * k).astype(_bf16), group, preferred_element_type=_f32)
            kmax_ref[s] = jnp.max(norms2, axis=0, keepdims=True)
    lane = lax.broadcasted_iota(jnp.int32, (1, LANES), 1)
    qt = qt_ref[0]
    row_head = (lax.broadcasted_iota(jnp.int32, (16, qt.shape[0]), 1) // HEAD_DIM
                == lax.broadcasted_iota(jnp.int32, (16, qt.shape[0]), 0)).astype(_bf16)
    qn2 = jnp.dot(row_head, qt * qt, preferred_element_type=_f32)
    for h in range(nh):
        kmax2 = jnp.max(jnp.where(lane == h % HEADS_PER_SLAB, kmax_ref[h // HEADS_PER_SLAB], 0.0),
                        axis=1, keepdims=True)
        bound_ref[h] = 1.05 * jnp.sqrt(qn2[h:h + 1] * kmax2) + UNDERFLOW_EXP2

    weights_stage = [name for (name, _, _) in _SB_STAGES].index("weights")
    late_heads = [h for h in range(nh) if h + _SB_GAP * weights_stage >= nh]
    ones16 = jnp.ones((16, KEY_BLOCK), _bf16)

    def pending_mass(kb):
        mass, kmean = {}, {}
        for h in late_heads:
            s = h // HEADS_PER_SLAB
            if s not in kmean:
                ksum = jnp.dot(ones16, _key_slab(k_ref, kb, h), preferred_element_type=_f32)
                kmean[s] = (ksum * (1.0 / KEY_BLOCK)).astype(_bf16)
            zbar = jnp.dot(kmean[s], qh[h], preferred_element_type=_f32)[0:1]
            margin = (bound_ref[h] - UNDERFLOW_EXP2) * 2.0 ** -7
            mass[h] = (KEY_BLOCK * (1.0 - 2.0 ** -8)) * _softplus2(zbar - margin)
        return mass

    def unfinished(mass):
        slack = None
        for h in range(nh):
            gap_h = later_ref[h] - bound_ref[h]
            if h in mass:
                gap_h = gap_h + mass[h]
            slack = gap_h if slack is None else jnp.minimum(slack, gap_h)
        return jnp.min(slack) < 0.0

    def body(carry):
        i, _ = carry
        mass = pending_mass(nfull - 1 - i)
        pipe.advance(_Block(nfull - i, None, handoff), _Block(nfull - 1 - i, None, handoff))
        return i + 1, unfinished(mass)

    n_done, _ = lax.while_loop(lambda c: jnp.logical_and(c[0] < nfull, c[1]), body,
                               (jnp.int32(0), unfinished({})))
    pipe.finish(_Block(nfull - n_done, None, handoff))
    o_ref[0] = jnp.concatenate([acc_ref[h] for h in range(nh)], axis=0).T.astype(o_ref.dtype)


def _prompt_attention(kernel_fn, name, qt, kbf, vt, ct, group):
    b, _, t = qt.shape
    nkb = t // KEY_BLOCK
    wstep = HEADS_PER_STEP * HEAD_DIM
    nstep = WIDTH // wstep
    off = group * nstep
    in_specs = [
        pl.BlockSpec((1, wstep, QUERY_BLOCK), lambda bi, p, j: (bi, off + p, j)),
        pl.BlockSpec((1, t, wstep), lambda bi, p, j: (bi, 0, off + p)),
        pl.BlockSpec((1, nkb, wstep, KEY_BLOCK), lambda bi, p, j: (bi, 0, off + p, 0)),
    ]
    args = [qt, kbf, vt]
    acc_rows = HEAD_DIM
    if ct is not None:
        in_specs.append(pl.BlockSpec((1, HEADS_PER_STEP, nkb, KEY_BLOCK), lambda bi, p, j: (bi, p, 0, 0)))
        args.append(ct)
        acc_rows += 16
    stages, gap = (_FOX_STAGES, _FOX_GAP) if group == 0 else (_SB_STAGES, _SB_GAP)
    scratch = [pltpu.VMEM((HEADS_PER_STEP, 1, QUERY_BLOCK), _f32),
               pltpu.VMEM((HEADS_PER_STEP, acc_rows, QUERY_BLOCK), _f32)]
    if group == 1:
        scratch += [pltpu.VMEM((HEADS_PER_STEP // HEADS_PER_SLAB, 1, LANES), _f32),
                    pltpu.VMEM((HEADS_PER_STEP, 1, QUERY_BLOCK), _f32)]
    scratch += [pltpu.VMEM(*_intermediate_type(name))
                for (name, _) in _handoff_keys(stages, HEADS_PER_STEP, gap)]
    return pl.pallas_call(
        kernel_fn,
        out_shape=jax.ShapeDtypeStruct((b, t, WIDTH), _bf16),
        grid=(b, nstep, t // QUERY_BLOCK),
        in_specs=in_specs,
        out_specs=pl.BlockSpec((1, QUERY_BLOCK, wstep), lambda bi, p, j: (bi, j, p)),
        scratch_shapes=scratch,
        compiler_params=pltpu.CompilerParams(
            dimension_semantics=("arbitrary", "arbitrary", "arbitrary"), vmem_limit_bytes=VMEM_LIMIT),
        name=name,
    )(*args)


def _merge_kernel(oa_ref, ob_ref, z_ref, x_ref, ga_ref, gb_ref, w_ref, fg_ref, out_ref, *, final):
    def gated(o, g, z):
        return (_rmsnorm(o, g) * (z * (1.0 / (1.0 + jnp.exp(-z))))).astype(_bf16)

    z = z_ref[0].astype(_f32)
    ya = gated(oa_ref[0].astype(_f32), ga_ref[...], z[:, 0:WIDTH])
    yb = gated(ob_ref[0].astype(_f32), gb_ref[...], z[:, WIDTH:2 * WIDTH])
    y = (jnp.dot(ya, w_ref[0:WIDTH, :], preferred_element_type=_f32)
         + jnp.dot(yb, w_ref[WIDTH:2 * WIDTH, :], preferred_element_type=_f32))
    xn = x_ref[0] + y
    out_ref[0] = _rmsnorm(xn, fg_ref[...]) if final else xn


def _merge(oa, ob, z, x, ga, gb, w_out, fg, final):
    b, t, d = x.shape
    tm = min(MERGE_ROWS, t)
    row = lambda bi, ti: (bi, ti, 0)
    const2 = lambda bi, ti: (0, 0)
    return pl.pallas_call(
        functools.partial(_merge_kernel, final=final),
        out_shape=jax.ShapeDtypeStruct((b, t, d), _f32),
        grid=(b, t // tm),
        in_specs=[
            pl.BlockSpec((1, tm, WIDTH), row),
            pl.BlockSpec((1, tm, WIDTH), row),
            pl.BlockSpec((1, tm, 2 * WIDTH), row),
            pl.BlockSpec((1, tm, d), row),
            pl.BlockSpec((1, WIDTH), const2),
            pl.BlockSpec((1, WIDTH), const2),
            pl.BlockSpec(w_out.shape, const2),
            pl.BlockSpec((1, d), const2),
        ],
        out_specs=pl.BlockSpec((1, tm, d), row),
        compiler_params=pltpu.CompilerParams(
            dimension_semantics=("arbitrary", "arbitrary"), vmem_limit_bytes=VMEM_LIMIT),
        name="merge_final" if final else "merge",
    )(oa, ob, z, x, ga, gb, w_out, fg)


def _sample_inproj_kernel(x_ref, g_ref, w_ref, wf_ref, bf_ref, u_ref, logf_ref, logft_ref):
    h = _rmsnorm(x_ref[...], g_ref[...]).astype(_bf16)
    u_ref[...] = jnp.dot(h, w_ref[...], preferred_element_type=_f32)
    fa = jnp.dot(h, wf_ref[...], preferred_element_type=_f32) + bf_ref[...]
    logf = _log_sigmoid(fa)
    logf_ref[...] = logf[:, 0:N_FORGET]
    logft_ref[...] = logf.T[0:N_FORGET, :]


def _sample_inproj(x2d, g, w_main, w_f, b_f):
    r, d = x2d.shape
    return pl.pallas_call(
        _sample_inproj_kernel,
        out_shape=(jax.ShapeDtypeStruct((r, w_main.shape[1]), _f32),
                   jax.ShapeDtypeStruct((r, N_FORGET), _f32),
                   jax.ShapeDtypeStruct((N_FORGET, r), _f32)),
        compiler_params=pltpu.CompilerParams(vmem_limit_bytes=VMEM_LIMIT),
        name="sample_inproj",
    )(x2d, g, w_main, w_f, b_f)


def _expand_heads(m, rows_per_head):
    return jnp.concatenate(
        [jnp.broadcast_to(m[h:h + 1, :], (rows_per_head, m.shape[1])) for h in range(N_HEADS)], axis=0)


def _nt_dot(a, b):
    return lax.dot_general(a, b, (((1,), (1,)), ((), ())), preferred_element_type=_f32)


def _head_cols(x, h):
    return x[:, h * HEAD_DIM:(h + 1) * HEAD_DIM]


def _scores_cached(q, kt_ref):
    return jnp.concatenate([jnp.dot(_head_cols(q, h), kt_ref[0, 0, h].astype(_bf16), preferred_element_type=_f32)
                            for h in range(N_HEADS)], axis=0)


def _values_cached(p, vt_ref, tq):
    return jnp.concatenate([_nt_dot(p[h * tq:(h + 1) * tq], vt_ref[0, 0, h].astype(_bf16))
                            for h in range(N_HEADS)], axis=0)


def _scores_new(q, knew):
    return jnp.concatenate([_nt_dot(_head_cols(q, h), _head_cols(knew, h)) for h in range(N_HEADS)], axis=0)


def _values_new(p, vnew, tq):
    return jnp.concatenate([jnp.dot(p[h * tq:(h + 1) * tq], _head_cols(vnew, h), preferred_element_type=_f32)
                            for h in range(N_HEADS)], axis=0)


def _heads_to_lanes(o, tq):
    return jnp.concatenate([o[h * tq:(h + 1) * tq] for h in range(N_HEADS)], axis=1)


def _sample_attn_kernel(u_ref, lfn_ref, lfc_ref, ka_ref, va_ref, kb_ref, vb_ref, tri_ref,
                        oa_ref, ob_ref,
                        cc_ref, xnew_ref, m_ref, l_ref, acca_ref, later_ref, accb_ref, *, tq, nch):
    i = pl.program_id(1)
    kc = SAMPLE_CHUNK
    rows = N_HEADS * tq
    u = u_ref[0]
    qa = (u[:, 0:WIDTH] * SCALE).astype(_bf16)
    qb = (u[:, 4 * WIDTH:5 * WIDTH] * SCALE).astype(_bf16)
    rq = lax.broadcasted_iota(jnp.int32, (rows, LANES), 0) % tq
    ln = lax.broadcasted_iota(jnp.int32, (rows, LANES), 1)
    pad = jnp.zeros((LANES - tq, WIDTH), _f32)

    @pl.when(i == 0)
    def _():
        carry = jnp.zeros((16, 1), _f32)
        zrows = jnp.zeros((16 - N_FORGET, kc), _f32)
        for ch in range(nch):
            x = jnp.concatenate([lfc_ref[0, :, ch * kc:(ch + 1) * kc], zrows], axis=0)
            cch = _lane_cumsum(x, carry)
            cc_ref[ch] = cch
            carry = cch[:, kc - 1:kc]
        xn = jnp.concatenate([lfn_ref[0], jnp.zeros((16 - N_FORGET, LANES), _f32)], axis=0)
        cnew = _lane_cumsum(xn, carry)
        xnew = _expand_heads(cnew[0:N_HEADS, :], tq)
        xnew_ref[...] = xnew
        cq = jnp.sum(jnp.where(ln == rq, xnew, 0.0), axis=1, keepdims=True)

        knew = jnp.concatenate([u[:, WIDTH:2 * WIDTH], pad], axis=0).astype(_bf16)
        vnew = jnp.concatenate([u[:, 2 * WIDTH:3 * WIDTH], pad], axis=0).astype(_bf16)
        logit = jnp.where(ln <= rq, _scores_new(qa, knew) + (cq - xnew), NEG_BIG)
        m = jnp.max(logit, axis=1, keepdims=True)
        p = jnp.exp(logit - m)
        m_ref[...] = m
        l_ref[...] = jnp.sum(p, axis=1, keepdims=True)
        acca_ref[...] = _values_new(p.astype(_bf16), vnew, tq)

        knew = jnp.concatenate([u[:, 5 * WIDTH:6 * WIDTH], pad], axis=0).astype(_bf16)
        vnew = jnp.concatenate([u[:, 6 * WIDTH:7 * WIDTH], pad], axis=0).astype(_bf16)
        z = _scores_new(qb, knew)
        msk = ln < rq
        sp = jnp.where(msk, _softplus(z), 0.0)
        r = lax.broadcasted_iota(jnp.int32, (2 * LANES, LANES), 0)
        c = lax.broadcasted_iota(jnp.int32, (2 * LANES, LANES), 1)
        r = jnp.where(r >= LANES, r - LANES, r)
        suffix = jnp.where(r > c, 1.0, 0.0).astype(_bf16)
        within = jnp.dot(_split2_cols(sp), suffix, preferred_element_type=_f32)
        a = jnp.where(msk, jnp.exp(z - sp - within), 0.0)
        accb_ref[...] = _values_new(a.astype(_bf16), vnew, tq)
        later_ref[...] = jnp.sum(sp, axis=1, keepdims=True)

    xq = xnew_ref[...]
    cq = jnp.sum(jnp.where(ln == rq, xq, 0.0), axis=1, keepdims=True)
    ck = _expand_heads(cc_ref[i][0:N_HEADS, :], tq)
    logit = _scores_cached(qa, ka_ref) + (cq - ck)
    m_old = m_ref[...]
    m_new = jnp.maximum(m_old, jnp.max(logit, axis=1, keepdims=True))
    alpha = jnp.exp(m_old - m_new)
    p = jnp.exp(logit - m_new)
    m_ref[...] = m_new
    l_ref[...] = alpha * l_ref[...] + jnp.sum(p, axis=1, keepdims=True)
    acca_ref[...] = alpha * acca_ref[...] + _values_cached(p.astype(_bf16), va_ref, tq)

    z = _scores_cached(qb, kb_ref)
    sp = _softplus(z)
    within = jnp.dot(_split2_cols(sp), tri_ref[...], preferred_element_type=_f32)
    a = jnp.exp(z - sp - within - later_ref[...])
    accb_ref[...] = accb_ref[...] + _values_cached(a.astype(_bf16), vb_ref, tq)
    later_ref[...] = later_ref[...] + jnp.sum(sp, axis=1, keepdims=True)

    @pl.when(i == nch - 1)
    def _():
        oa_ref[0] = _heads_to_lanes(acca_ref[...] / l_ref[...], tq)
        ob_ref[0] = _heads_to_lanes(accb_ref[...], tq)


def _sample_attention(layer, u, lf_new, lf_cache_t, ck_a, cv_a, ck_b, cv_b, tri):
    nb, tq, _ = u.shape
    past = ck_a.shape[4]
    kc = SAMPLE_CHUNK
    nch = past // kc
    rows = N_HEADS * tq
    chunk = (1, 1, N_HEADS, HEAD_DIM, kc)
    fwd = lambda bi, i: (layer, bi, 0, 0, i)
    rev = lambda bi, i: (layer, bi, 0, 0, nch - 1 - i)
    per_b = lambda bi, i: (bi, 0, 0)
    return pl.pallas_call(
        functools.partial(_sample_attn_kernel, tq=tq, nch=nch),
        out_shape=(jax.ShapeDtypeStruct((nb, tq, WIDTH), _f32),
                   jax.ShapeDtypeStruct((nb, tq, WIDTH), _f32)),
        grid=(nb, nch),
        in_specs=[
            pl.BlockSpec((1, tq, u.shape[2]), per_b),
            pl.BlockSpec((1, N_FORGET, LANES), per_b),
            pl.BlockSpec((1, N_FORGET, past), per_b),
            pl.BlockSpec(chunk, fwd),
            pl.BlockSpec(chunk, fwd),
            pl.BlockSpec(chunk, rev),
            pl.BlockSpec(chunk, rev),
            pl.BlockSpec(tri.shape, lambda bi, i: (0, 0)),
        ],
        out_specs=(pl.BlockSpec((1, tq, WIDTH), per_b), pl.BlockSpec((1, tq, WIDTH), per_b)),
        scratch_shapes=[
            pltpu.VMEM((nch, 16, kc), _f32),
            pltpu.VMEM((rows, LANES), _f32),
            pltpu.VMEM((rows, 1), _f32),
            pltpu.VMEM((rows, 1), _f32),
            pltpu.VMEM((rows, HEAD_DIM), _f32),
            pltpu.VMEM((rows, 1), _f32),
            pltpu.VMEM((rows, HEAD_DIM), _f32),
        ],
        compiler_params=pltpu.CompilerParams(
            dimension_semantics=("arbitrary", "arbitrary"), vmem_limit_bytes=VMEM_LIMIT),
        name="sample_attention",
    )(u, lf_new, lf_cache_t, ck_a, cv_a, ck_b, cv_b, tri)


def _split_weights(w):
    f0 = 3 * WIDTH
    w_main = jnp.concatenate([w[:, 0:f0], w[:, f0 + N_FORGET:]], axis=1).astype(_bf16)
    w_f = jnp.pad(w[:, f0:f0 + N_FORGET], ((0, 0), (0, LANES - N_FORGET))).astype(_bf16)
    return w_main, w_f


def kernel(x_prompt, x_sample, cache_fox_k, cache_fox_v, cache_fox_logf, cache_sb_k, cache_sb_v,
           norm_g, w_in, b_f, out_norm_a, out_norm_b, w_out, final_norm_g):
    depth = norm_g.shape[0]
    b, t, d = x_prompt.shape
    nb, tq, _ = x_sample.shape
    past = cache_fox_k.shape[2]
    nkb = t // KEY_BLOCK
    assert t % QUERY_BLOCK == 0 and t % PROJ_ROWS == 0 and past % SAMPLE_CHUNK == 0
    assert tq <= 16 and (nb * tq) % 8 == 0

    kc = SAMPLE_CHUNK
    r = lax.broadcasted_iota(jnp.int32, (2 * kc, kc), 0) % kc
    c = lax.broadcasted_iota(jnp.int32, (2 * kc, kc), 1)
    tri = jnp.where(r > c, 1.0, 0.0).astype(_bf16)
    fg = final_norm_g.reshape(1, d)
    time_minor = lambda a: a.transpose(0, 1, 3, 4, 2)
    cache_fox_k, cache_fox_v = time_minor(cache_fox_k), time_minor(cache_fox_v)
    cache_sb_k, cache_sb_v = time_minor(cache_sb_k), time_minor(cache_sb_v)

    xp, xs = x_prompt, x_sample.reshape(1, nb * tq, d)
    stacked, sample_new = None, []
    for l in range(depth):
        w_main, w_f = _split_weights(w_in[l])
        g = norm_g[l].reshape(1, d)
        bf = jnp.pad(b_f[l], (0, LANES - N_FORGET)).reshape(1, LANES)
        ga, gb = out_norm_a[l].reshape(1, WIDTH), out_norm_b[l].reshape(1, WIDTH)
        wo = w_out[l].astype(_bf16)
        final = l == depth - 1

        outs = _inproj(l, depth, stacked, xp, g, w_main, w_f, bf)
        stacked, (ct, qt, kbf, vt, z) = outs[:N_STACKED], outs[N_STACKED:]
        ct = ct.reshape(b, N_HEADS, nkb, KEY_BLOCK)
        oa = _prompt_attention(_fox_kernel, "fox_attention", qt, kbf, vt, ct, 0)
        ob = _prompt_attention(_sb_kernel, "sb_attention", qt, kbf, vt, None, 1)
        xp = _merge(oa, ob, z, xp, ga, gb, wo, fg, final)

        u, logf_s, logft_s = _sample_inproj(xs[0], g, w_main, w_f, bf)
        u3 = u.reshape(nb, tq, u.shape[1])
        lf_new = jnp.pad(logft_s.reshape(N_FORGET, nb, tq).transpose(1, 0, 2),
                         ((0, 0), (0, 0), (0, LANES - tq)))
        lf_cache_t = cache_fox_logf[l].transpose(0, 2, 1)
        oa_s, ob_s = _sample_attention(l, u3, lf_new, lf_cache_t,
                                       cache_fox_k, cache_fox_v, cache_sb_k, cache_sb_v, tri)
        z_s = jnp.concatenate([u[:, 3 * WIDTH:4 * WIDTH], u[:, 7 * WIDTH:8 * WIDTH]], axis=1)
        xs = _merge(oa_s.reshape(1, nb * tq, WIDTH), ob_s.reshape(1, nb * tq, WIDTH),
                    z_s.reshape(1, nb * tq, 2 * WIDTH), xs, ga, gb, wo, fg, final)
        hs = lambda a: a.reshape(nb, tq, N_HEADS, HEAD_DIM)
        sample_new.append((hs(u[:, WIDTH:2 * WIDTH]), hs(u[:, 2 * WIDTH:3 * WIDTH]),
                           logf_s.reshape(nb, tq, N_FORGET),
                           hs(u[:, 5 * WIDTH:6 * WIDTH]), hs(u[:, 6 * WIDTH:7 * WIDTH])))

    heads = lambda a: a.reshape(depth, b, N_HEADS, HEAD_DIM, t).transpose(0, 1, 4, 2, 3)
    kat, vat, kbt, vbt, lft = stacked
    stack = lambda items, i: jnp.stack([it[i] for it in items])
    return (xp, xs.reshape(nb, tq, d),
            heads(kat), heads(vat), lft.transpose(0, 1, 3, 2), heads(kbt), heads(vbt),
            *(stack(sample_new, i) for i in range(5)))
```

```python
import functools

import jax
import jax.numpy as jnp
from jax import lax
from jax.experimental import pallas as pl
from jax.experimental.pallas import tpu as pltpu

HEAD_DIM = 64
N_HEADS = 8
WIDTH = N_HEADS * HEAD_DIM
N_FORGET = N_HEADS
EPS = 1e-6
SCALE = HEAD_DIM ** -0.5
LOG2E = 1.4426950408889634
NEG_BIG = -1e30
UNDERFLOW_EXP2 = 150.0

LANES = 128
KEY_BLOCK = 256
QUERY_BLOCK = 256
SLAB = 256
HEADS_PER_SLAB = SLAB // HEAD_DIM
PROJ_ROWS = 256
MERGE_ROWS = 512
SAMPLE_CHUNK = 512
VMEM_LIMIT = 48 * 1024 * 1024

_f32 = jnp.float32
_bf16 = jnp.bfloat16


def _softplus(z):
    neg_abs = lax.bitcast_convert_type(
        lax.bitcast_convert_type(z, jnp.uint32) | jnp.uint32(0x80000000), _f32)
    return jnp.maximum(z, 0.0) + jnp.log(1.0 + jnp.exp(neg_abs))


def _softplus2(zs):
    neg_abs = lax.bitcast_convert_type(
        lax.bitcast_convert_type(zs, jnp.uint32) | jnp.uint32(0x80000000), _f32)
    return jnp.maximum(zs, 0.0) + jnp.log(1.0 + jnp.exp2(neg_abs)) * LOG2E


def _log_sigmoid(x):
    return jnp.minimum(x, 0.0) - jnp.log(1.0 + jnp.exp(-jnp.abs(x)))


def _rmsnorm(x, g):
    return x * lax.rsqrt(jnp.mean(x * x, axis=-1, keepdims=True) + EPS) * g


def _split3(x):
    a1 = x.astype(_bf16)
    r1 = x - a1.astype(_f32)
    a2 = r1.astype(_bf16)
    a3 = (r1 - a2.astype(_f32)).astype(_bf16)
    return a1, a2, a3


def _split2_cols(x):
    hi = lax.bitcast_convert_type(
        lax.bitcast_convert_type(x, jnp.uint32) & jnp.uint32(0xFFFF0000), _f32)
    return jnp.concatenate([hi.astype(_bf16), (x - hi).astype(_bf16)], axis=1)


def _lane_cumsum(x, carry):
    n = x.shape[1]
    r = lax.broadcasted_iota(jnp.int32, (n, n), 0)
    c = lax.broadcasted_iota(jnp.int32, (n, n), 1)
    tri = jnp.where(r <= c, 1.0, 0.0).astype(_bf16)
    a1, a2, a3 = _split3(x)
    dot = functools.partial(jnp.dot, preferred_element_type=_f32)
    return carry + (dot(a1, tri) + dot(a2, tri) + dot(a3, tri))


N_STACKED = 5


def _inproj_kernel(x_ref, g_ref, w_ref, wf_ref, bf_ref, *refs, first):
    if not first:
        refs = refs[N_STACKED:]
    kat_ref, vat_ref, kbt_ref, vbt_ref, lft_ref, ct_ref, qt_ref, kbf_ref, vt_ref, z_ref, carry_ref = refs
    tm = x_ref.shape[1]
    h = _rmsnorm(x_ref[0], g_ref[...]).astype(_bf16)

    def seg(i):
        return jnp.dot(h, w_ref[:, i * WIDTH:(i + 1) * WIDTH], preferred_element_type=_f32)

    def put_stacked(ref, value):
        for layer in range(ref.shape[0]):
            ref[layer, 0] = value

    for grp, (kt_ref, vtf_ref) in enumerate(((kat_ref, vat_ref), (kbt_ref, vbt_ref))):
        lo, hi = grp * WIDTH, (grp + 1) * WIDTH
        q = seg(4 * grp + 0) * (SCALE * LOG2E)
        qt_ref[0, lo:hi, :] = q.T.astype(_bf16)
        k = seg(4 * grp + 1)
        put_stacked(kt_ref, k.T)
        kbf_ref[0, :, lo:hi] = k.astype(_bf16)
        vt = seg(4 * grp + 2).T
        put_stacked(vtf_ref, vt)
        for s in range(tm // KEY_BLOCK):
            vt_ref[0, s, lo:hi, :] = vt[:, s * KEY_BLOCK:(s + 1) * KEY_BLOCK].astype(_bf16)
        z_ref[0, :, lo:hi] = seg(4 * grp + 3).astype(z_ref.dtype)

    fa = jnp.dot(h, wf_ref[...], preferred_element_type=_f32) + bf_ref[...]
    logft = _log_sigmoid(fa).T[0:16, :]
    put_stacked(lft_ref, logft[0:N_FORGET, :])

    @pl.when(pl.program_id(1) == 0)
    def _():
        carry_ref[...] = jnp.zeros_like(carry_ref)

    c = _lane_cumsum(logft, carry_ref[:, 0:1])
    ct_ref[0] = c[0:N_FORGET, :] * LOG2E
    carry_ref[...] = jnp.broadcast_to(c[:, tm - 1:tm], carry_ref.shape)


def _inproj(layer, depth, stacked, x, g, w_main, w_f, b_f):
    b, t, d = x.shape
    tm = PROJ_ROWS
    nkb = t // KEY_BLOCK
    first = stacked is None
    row = lambda bi, ti: (bi, ti, 0)
    const2 = lambda bi, ti: (0, 0)
    feat = lambda bi, ti: (bi, 0, ti)
    widths = (WIDTH, WIDTH, WIDTH, WIDTH, N_FORGET)
    slabs = depth if first else 1
    slab = lambda bi, ti: (0 if first else layer, bi, 0, ti)
    out_shape = tuple(jax.ShapeDtypeStruct((depth, b, w, t), _f32) for w in widths) + (
        jax.ShapeDtypeStruct((b, N_FORGET, t), _f32),
        jax.ShapeDtypeStruct((b, 2 * WIDTH, t), _bf16),
        jax.ShapeDtypeStruct((b, t, 2 * WIDTH), _bf16),
        jax.ShapeDtypeStruct((b, nkb, 2 * WIDTH, KEY_BLOCK), _bf16),
        jax.ShapeDtypeStruct((b, t, 2 * WIDTH), _bf16),
    )
    out_specs = tuple(pl.BlockSpec((slabs, 1, w, tm), slab) for w in widths) + (
        pl.BlockSpec((1, N_FORGET, tm), feat),
        pl.BlockSpec((1, 2 * WIDTH, tm), feat),
        pl.BlockSpec((1, tm, 2 * WIDTH), row),
        pl.BlockSpec((1, tm // KEY_BLOCK, 2 * WIDTH, KEY_BLOCK), lambda bi, ti: (bi, ti, 0, 0)),
        pl.BlockSpec((1, tm, 2 * WIDTH), row),
    )
    in_specs = [
        pl.BlockSpec((1, tm, d), row),
        pl.BlockSpec((1, d), const2),
        pl.BlockSpec(w_main.shape, const2),
        pl.BlockSpec(w_f.shape, const2),
        pl.BlockSpec((1, LANES), const2),
    ]
    args = [x, g, w_main, w_f, b_f]
    aliases = {}
    if not first:
        in_specs += [pl.BlockSpec(memory_space=pl.ANY)] * N_STACKED
        aliases = {len(args) + i: i for i in range(N_STACKED)}
        args += list(stacked)
    return pl.pallas_call(
        functools.partial(_inproj_kernel, first=first),
        out_shape=out_shape,
        grid=(b, t // tm),
        in_specs=in_specs,
        out_specs=out_specs,
        scratch_shapes=[pltpu.VMEM((16, LANES), _f32)],
        input_output_aliases=aliases,
        compiler_params=pltpu.CompilerParams(
            dimension_semantics=("arbitrary", "arbitrary"), vmem_limit_bytes=VMEM_LIMIT),
        name="prompt_inproj",
    )(*args)


def _head_queries(qt_ref):
    out = []
    for s in range(qt_ref.shape[1] // SLAB):
        qt = qt_ref[0, s * SLAB:(s + 1) * SLAB, :]
        head = lax.broadcasted_iota(jnp.int32, qt.shape, 0) // HEAD_DIM
        out += [jnp.where(head == hh, qt, jnp.zeros_like(qt)) for hh in range(SLAB // HEAD_DIM)]
    return out


_FOX_STAGES = (("scores", (), ("s",)), ("logits", ("s",), ("logit", "mnew", "alpha")),
               ("weights", ("logit", "mnew"), ("p",)), ("values", ("p", "alpha"), ()))
_SB_STAGES = (("scores", (), ("z",)), ("survival", ("z",), ("sp",)), ("suffix_sums", ("sp",), ("w",)),
              ("weights", ("z", "w"), ("a",)), ("values", ("a",), ()))


def _intermediate_type(name):
    tile = (KEY_BLOCK, QUERY_BLOCK)
    row = (1, QUERY_BLOCK)
    return {"s": (tile, _f32), "z": (tile, _f32), "logit": (tile, _f32), "p": (tile, _bf16), "sp": (tile, _bf16),
            "a": (tile, _bf16), "alpha": (row, _f32), "mnew": (row, _f32),
            "w": ((KEY_BLOCK + 16, QUERY_BLOCK), _f32)}[name]


class _Block:
    def __init__(self, kb, mask, handoff):
        self.kb, self.mask, self.handoff, self.vals = kb, mask, handoff, {}

    def put(self, name, h, value):
        self.vals[(name, h)] = value
        if (name, h) in self.handoff:
            self.handoff[(name, h)][...] = value

    def get(self, name, h):
        if (name, h) in self.vals:
            return self.vals[(name, h)]
        return self.handoff[(name, h)][...]


def _key_slab(k_ref, kb, h):
    s = h // HEADS_PER_SLAB
    return k_ref[0, pl.ds(pl.multiple_of(kb * KEY_BLOCK, KEY_BLOCK), KEY_BLOCK), s * SLAB:(s + 1) * SLAB]


_MIX_STEPS = 2 * N_HEADS
_MIX_ITEMS = ([(("sb", h), 2 * h, 2, _SB_STAGES) for h in range(N_HEADS)]
              + [(("fox", h), 2 * h + 1, 4, _FOX_STAGES) for h in range(N_HEADS)])


def _mix_handoff_keys():
    keys = []
    for (item, pos, gap, stages) in _MIX_ITEMS:
        done = min(len(stages), (_MIX_STEPS - 1 - pos) // gap + 1)
        made = {n for (_, _, w) in stages[:done] for n in w}
        need = {n for (_, r, _) in stages[done:] for n in r}
        keys += [(n, item) for n in sorted(made & need)]
    return keys


class _ItemPipeline:
    def __init__(self, fns):
        self.items = [(item, pos, gap, [fns[item[0]][name] for (name, _, _) in stages])
                      for (item, pos, gap, stages) in _MIX_ITEMS]
        self.n = _MIX_STEPS
        self.tail = max(pos + gap * (len(f) - 1) for (_, pos, gap, f) in self.items) + 1 - self.n
        assert 0 <= self.tail <= self.n

    def _step(self, blk, step, mixers):
        for (item, pos, gap, fns) in self.items:
            k, rem = divmod(step - pos, gap)
            if item[0] in mixers and step >= pos and rem == 0 and k < len(fns):
                fns[k](blk, item)

    def advance(self, prev, cur, prev_mixers=("fox", "sb"), cur_mixers=("fox", "sb")):
        for step in range(self.n):
            if prev is not None and step < self.tail:
                self._step(prev, self.n + step, prev_mixers)
            self._step(cur, step, cur_mixers)

    def finish(self, prev, mixers=("fox", "sb")):
        for step in range(self.tail):
            self._step(prev, self.n + step, mixers)


def _attention_kernel(qt_ref, k_ref, vt_ref, ct_ref, o_ref,
                      m_ref, accf_ref, later_ref, accs_ref, kmax_ref, bound_ref, *handoff_refs):
    j = pl.program_id(1)
    nh = N_HEADS
    handoff = dict(zip(_mix_handoff_keys(), handoff_refs))
    qh = _head_queries(qt_ref)
    nfull = j
    head = {"fox": lambda h: h, "sb": lambda h: nh + h}
    kio = lax.broadcasted_iota(jnp.int32, (KEY_BLOCK, QUERY_BLOCK), 0)
    qio = lax.broadcasted_iota(jnp.int32, (KEY_BLOCK, QUERY_BLOCK), 1)
    diag_mask = {"fox": kio <= qio, "sb": kio < qio}
    c0 = [ct_ref[0, h, pl.ds(nfull, 1), :][:, 0:1] for h in range(nh)]
    ones_rows = (lax.broadcasted_iota(jnp.int32, (16, KEY_BLOCK), 0) == 0).astype(_bf16)
    r = lax.broadcasted_iota(jnp.int32, (KEY_BLOCK + 16, KEY_BLOCK), 0)
    c = lax.broadcasted_iota(jnp.int32, (KEY_BLOCK + 16, KEY_BLOCK), 1)
    suffix = jnp.where((c >= r) | (r == KEY_BLOCK), 1.0, 0.0).astype(_bf16)

    m_ref[...] = jnp.full(m_ref.shape, NEG_BIG, _f32)
    accf_ref[...] = jnp.zeros(accf_ref.shape, _f32)
    later_ref[...] = jnp.zeros(later_ref.shape, _f32)
    accs_ref[...] = jnp.zeros(accs_ref.shape, _f32)

    def scores(name):
        def stage(blk, item):
            g = head[item[0]](item[1])
            s = jnp.dot(_key_slab(k_ref, blk.kb, g), qh[g], preferred_element_type=_f32)
            blk.put(name, item, s if blk.mask is None else jnp.where(blk.mask[item[0]], s, NEG_BIG))
        return stage

    def value_rows(blk, item):
        g = head[item[0]](item[1])
        return vt_ref[0, blk.kb, g * HEAD_DIM:(g + 1) * HEAD_DIM, :]

    def fox_logits(blk, item):
        h = item[1]
        ck_row = ct_ref[0, h, pl.ds(blk.kb, 1), :] - c0[h]
        ck = jnp.broadcast_to(ck_row, (LANES, KEY_BLOCK)).T
        ck = jnp.concatenate([ck] * (QUERY_BLOCK // LANES), axis=1)
        logit = blk.get("s", item) - ck
        m_old = m_ref[h]
        m_new = jnp.maximum(m_old, jnp.max(logit, axis=0, keepdims=True))
        blk.put("logit", item, logit)
        blk.put("mnew", item, m_new)
        blk.put("alpha", item, jnp.exp2(m_old - m_new))
        m_ref[h] = m_new

    def fox_weights(blk, item):
        blk.put("p", item, jnp.exp2(blk.get("logit", item) - blk.get("mnew", item)).astype(_bf16))

    def fox_values(blk, item):
        h = item[1]
        vth = jnp.concatenate([value_rows(blk, item), ones_rows], axis=0)
        accf_ref[h] = (blk.get("alpha", item) * accf_ref[h]
                       + jnp.dot(vth, blk.get("p", item), preferred_element_type=_f32))

    def sb_survival(blk, item):
        blk.put("sp", item, _softplus2(blk.get("z", item)).astype(_bf16))

    def sb_suffix_sums(blk, item):
        blk.put("w", item, jnp.dot(suffix, blk.get("sp", item), preferred_element_type=_f32))

    def sb_weights(blk, item):
        h = item[1]
        w = blk.get("w", item)
        later = later_ref[h]
        blk.put("a", item, jnp.exp2(blk.get("z", item) - w[0:KEY_BLOCK] - later).astype(_bf16))
        later_ref[h] = later + w[KEY_BLOCK:KEY_BLOCK + 1]

    def sb_values(blk, item):
        h = item[1]
        accs_ref[h] = accs_ref[h] + jnp.dot(value_rows(blk, item), blk.get("a", item),
                                            preferred_element_type=_f32)

    pipe = _ItemPipeline({
        "fox": {"scores": scores("s"), "logits": fox_logits, "weights": fox_weights, "values": fox_values},
        "sb": {"scores": scores("z"), "survival": sb_survival, "suffix_sums": sb_suffix_sums,
               "weights": sb_weights, "values": sb_values}})
    pipe.advance(None, _Block(nfull, diag_mask, handoff))

    sb_slab0 = nh // HEADS_PER_SLAB

    @pl.when(j == 0)
    def _():
        lane_head = lax.broadcasted_iota(jnp.int32, (SLAB, LANES), 0) // HEAD_DIM
        group = (lane_head == lax.broadcasted_iota(jnp.int32, (SLAB, LANES), 1)).astype(_bf16)
        for s in range(nh // HEADS_PER_SLAB):
            k = k_ref[0, :, (sb_slab0 + s) * SLAB:(sb_slab0 + s + 1) * SLAB].astype(_f32)
            norms2 = jnp.dot((k * k).astype(_bf16), group, preferred_element_type=_f32)
            kmax_ref[s] = jnp.max(norms2, axis=0, keepdims=True)
    lane = lax.broadcasted_iota(jnp.int32, (1, LANES), 1)
    qsb = qt_ref[0, nh * HEAD_DIM:2 * nh * HEAD_DIM, :]
    row_head = (lax.broadcasted_iota(jnp.int32, (16, qsb.shape[0]), 1) // HEAD_DIM
                == lax.broadcasted_iota(jnp.int32, (16, qsb.shape[0]), 0)).astype(_bf16)
    qn2 = jnp.dot(row_head, qsb * qsb, preferred_element_type=_f32)
    for h in range(nh):
        kmax2 = jnp.max(jnp.where(lane == h % HEADS_PER_SLAB, kmax_ref[h // HEADS_PER_SLAB], 0.0),
                        axis=1, keepdims=True)
        bound_ref[h] = 1.05 * jnp.sqrt(qn2[h:h + 1] * kmax2) + UNDERFLOW_EXP2

    weights_stage = [name for (name, _, _) in _SB_STAGES].index("weights")
    late_heads = [item[1] for (item, pos, gap, _) in _MIX_ITEMS
                  if item[0] == "sb" and pos + gap * weights_stage >= _MIX_STEPS]
    ones16 = jnp.ones((16, KEY_BLOCK), _bf16)

    def pending_mass(kb):
        mass, kmean = {}, {}
        for h in late_heads:
            g = nh + h
            s = g // HEADS_PER_SLAB
            if s not in kmean:
                ksum = jnp.dot(ones16, _key_slab(k_ref, kb, g), preferred_element_type=_f32)
                kmean[s] = (ksum * (1.0 / KEY_BLOCK)).astype(_bf16)
            zbar = jnp.dot(kmean[s], qh[g], preferred_element_type=_f32)[0:1]
            margin = (bound_ref[h] - UNDERFLOW_EXP2) * 2.0 ** -7
            mass[h] = (KEY_BLOCK * (1.0 - 2.0 ** -8)) * _softplus2(zbar - margin)
        return mass

    def unfinished(mass):
        slack = None
        for h in range(nh):
            gap_h = later_ref[h] - bound_ref[h]
            if h in mass:
                gap_h = gap_h + mass[h]
            slack = gap_h if slack is None else jnp.minimum(slack, gap_h)
        return jnp.min(slack) < 0.0

    def both(carry):
        i, _ = carry
        mass = pending_mass(nfull - 1 - i)
        pipe.advance(_Block(nfull - i, None, handoff), _Block(nfull - 1 - i, None, handoff))
        return i + 1, unfinished(mass)

    n_both, _ = lax.while_loop(lambda cr: jnp.logical_and(cr[0] < nfull, cr[1]), both,
                               (jnp.int32(0), unfinished({})))

    @pl.when(n_both == nfull)
    def _():
        pipe.finish(_Block(0, None, handoff))

    @pl.when(n_both < nfull)
    def _():
        pipe.advance(_Block(nfull - n_both, None, handoff), _Block(nfull - 1 - n_both, None, handoff),
                     cur_mixers=("fox",))

        @pl.loop(n_both + 1, nfull)
        def _(i):
            pipe.advance(_Block(nfull - i, None, handoff), _Block(nfull - 1 - i, None, handoff),
                         prev_mixers=("fox",), cur_mixers=("fox",))

        pipe.finish(_Block(0, None, handoff), mixers=("fox",))

    o = [accf_ref[h, 0:HEAD_DIM] / accf_ref[h, HEAD_DIM:HEAD_DIM + 1] for h in range(nh)]
    o += [accs_ref[h] for h in range(nh)]
    o_ref[0] = jnp.concatenate(o, axis=0).T.astype(o_ref.dtype)


def _attention(qt, kbf, vt, ct):
    b, _, t = qt.shape
    nkb = t // KEY_BLOCK
    assert QUERY_BLOCK == KEY_BLOCK
    scratch = [pltpu.VMEM((N_HEADS, 1, QUERY_BLOCK), _f32),
               pltpu.VMEM((N_HEADS, HEAD_DIM + 16, QUERY_BLOCK), _f32),
               pltpu.VMEM((N_HEADS, 1, QUERY_BLOCK), _f32),
               pltpu.VMEM((N_HEADS, HEAD_DIM, QUERY_BLOCK), _f32),
               pltpu.VMEM((N_HEADS // HEADS_PER_SLAB, 1, LANES), _f32),
               pltpu.VMEM((N_HEADS, 1, QUERY_BLOCK), _f32)]
    scratch += [pltpu.VMEM(*_intermediate_type(name)) for (name, _) in _mix_handoff_keys()]
    return pl.pallas_call(
        _attention_kernel,
        out_shape=jax.ShapeDtypeStruct((b, t, 2 * WIDTH), _bf16),
        grid=(b, t // QUERY_BLOCK),
        in_specs=[
            pl.BlockSpec((1, 2 * WIDTH, QUERY_BLOCK), lambda bi, j: (bi, 0, j)),
            pl.BlockSpec((1, t, 2 * WIDTH), lambda bi, j: (bi, 0, 0)),
            pl.BlockSpec((1, nkb, 2 * WIDTH, KEY_BLOCK), lambda bi, j: (bi, 0, 0, 0)),
            pl.BlockSpec((1, N_HEADS, nkb, KEY_BLOCK), lambda bi, j: (bi, 0, 0, 0)),
        ],
        out_specs=pl.BlockSpec((1, QUERY_BLOCK, 2 * WIDTH), lambda bi, j: (bi, j, 0)),
        scratch_shapes=scratch,
        compiler_params=pltpu.CompilerParams(
            dimension_semantics=("arbitrary", "arbitrary"), vmem_limit_bytes=VMEM_LIMIT),
        name="attention",
    )(qt, kbf, vt, ct)


def _merge_kernel(o_ref, z_ref, x_ref, ga_ref, gb_ref, w_ref, fg_ref, out_ref, *, final):
    def gated(o, g, z):
        return (_rmsnorm(o, g) * (z * (1.0 / (1.0 + jnp.exp(-z))))).astype(_bf16)

    z = z_ref[0].astype(_f32)
    o = o_ref[0].astype(_f32)
    ya = gated(o[:, 0:WIDTH], ga_ref[...], z[:, 0:WIDTH])
    yb = gated(o[:, WIDTH:2 * WIDTH], gb_ref[...], z[:, WIDTH:2 * WIDTH])
    y = (jnp.dot(ya, w_ref[0:WIDTH, :], preferred_element_type=_f32)
         + jnp.dot(yb, w_ref[WIDTH:2 * WIDTH, :], preferred_element_type=_f32))
    xn = x_ref[0] + y
    out_ref[0] = _rmsnorm(xn, fg_ref[...]) if final else xn


def _merge(o, z, x, ga, gb, w_out, fg, final):
    b, t, d = x.shape
    tm = min(MERGE_ROWS, t)
    row = lambda bi, ti: (bi, ti, 0)
    const2 = lambda bi, ti: (0, 0)
    return pl.pallas_call(
        functools.partial(_merge_kernel, final=final),
        out_shape=jax.ShapeDtypeStruct((b, t, d), _f32),
        grid=(b, t // tm),
        in_specs=[
            pl.BlockSpec((1, tm, 2 * WIDTH), row),
            pl.BlockSpec((1, tm, 2 * WIDTH), row),
            pl.BlockSpec((1, tm, d), row),
            pl.BlockSpec((1, WIDTH), const2),
            pl.BlockSpec((1, WIDTH), const2),
            pl.BlockSpec(w_out.shape, const2),
            pl.BlockSpec((1, d), const2),
        ],
        out_specs=pl.BlockSpec((1, tm, d), row),
        compiler_params=pltpu.CompilerParams(
            dimension_semantics=("arbitrary", "arbitrary"), vmem_limit_bytes=VMEM_LIMIT),
        name="merge_final" if final else "merge",
    )(o, z, x, ga, gb, w_out, fg)


def _sample_inproj_kernel(x_ref, g_ref, w_ref, wf_ref, bf_ref, u_ref, logf_ref, logft_ref):
    h = _rmsnorm(x_ref[...], g_ref[...]).astype(_bf16)
    u_ref[...] = jnp.dot(h, w_ref[...], preferred_element_type=_f32)
    fa = jnp.dot(h, wf_ref[...], preferred_element_type=_f32) + bf_ref[...]
    logf = _log_sigmoid(fa)
    logf_ref[...] = logf[:, 0:N_FORGET]
    logft_ref[...] = logf.T[0:N_FORGET, :]


def _sample_inproj(x2d, g, w_main, w_f, b_f):
    r, d = x2d.shape
    return pl.pallas_call(
        _sample_inproj_kernel,
        out_shape=(jax.ShapeDtypeStruct((r, w_main.shape[1]), _f32),
                   jax.ShapeDtypeStruct((r, N_FORGET), _f32),
                   jax.ShapeDtypeStruct((N_FORGET, r), _f32)),
        compiler_params=pltpu.CompilerParams(vmem_limit_bytes=VMEM_LIMIT),
        name="sample_inproj",
    )(x2d, g, w_main, w_f, b_f)


def _expand_heads(m, rows_per_head):
    return jnp.concatenate(
        [jnp.broadcast_to(m[h:h + 1, :], (rows_per_head, m.shape[1])) for h in range(N_HEADS)], axis=0)


def _nt_dot(a, b):
    return lax.dot_general(a, b, (((1,), (1,)), ((), ())), preferred_element_type=_f32)


def _head_cols(x, h):
    return x[:, h * HEAD_DIM:(h + 1) * HEAD_DIM]


def _scores_cached(q, kt_ref):
    return jnp.concatenate([jnp.dot(_head_cols(q, h), kt_ref[0, 0, h].astype(_bf16), preferred_element_type=_f32)
                            for h in range(N_HEADS)], axis=0)


def _values_cached(p, vt_ref, tq):
    return jnp.concatenate([_nt_dot(p[h * tq:(h + 1) * tq], vt_ref[0, 0, h].astype(_bf16))
                            for h in range(N_HEADS)], axis=0)


def _scores_new(q, knew):
    return jnp.concatenate([_nt_dot(_head_cols(q, h), _head_cols(knew, h)) for h in range(N_HEADS)], axis=0)


def _values_new(p, vnew, tq):
    return jnp.concatenate([jnp.dot(p[h * tq:(h + 1) * tq], _head_cols(vnew, h), preferred_element_type=_f32)
                            for h in range(N_HEADS)], axis=0)


def _heads_to_lanes(o, tq):
    return jnp.concatenate([o[h * tq:(h + 1) * tq] for h in range(N_HEADS)], axis=1)


def _sample_attn_kernel(u_ref, lfn_ref, lfc_ref, ka_ref, va_ref, kb_ref, vb_ref, tri_ref,
                        oa_ref, ob_ref,
                        cc_ref, xnew_ref, m_ref, l_ref, acca_ref, later_ref, accb_ref, *, tq, nch):
    i = pl.program_id(1)
    kc = SAMPLE_CHUNK
    rows = N_HEADS * tq
    u = u_ref[0]
    qa = (u[:, 0:WIDTH] * SCALE).astype(_bf16)
    qb = (u[:, 4 * WIDTH:5 * WIDTH] * SCALE).astype(_bf16)
    rq = lax.broadcasted_iota(jnp.int32, (rows, LANES), 0) % tq
    ln = lax.broadcasted_iota(jnp.int32, (rows, LANES), 1)
    pad = jnp.zeros((LANES - tq, WIDTH), _f32)

    @pl.when(i == 0)
    def _():
        carry = jnp.zeros((16, 1), _f32)
        zrows = jnp.zeros((16 - N_FORGET, kc), _f32)
        for ch in range(nch):
            x = jnp.concatenate([lfc_ref[0, :, ch * kc:(ch + 1) * kc], zrows], axis=0)
            cch = _lane_cumsum(x, carry)
            cc_ref[ch] = cch
            carry = cch[:, kc - 1:kc]
        xn = jnp.concatenate([lfn_ref[0], jnp.zeros((16 - N_FORGET, LANES), _f32)], axis=0)
        cnew = _lane_cumsum(xn, carry)
        xnew = _expand_heads(cnew[0:N_HEADS, :], tq)
        xnew_ref[...] = xnew
        cq = jnp.sum(jnp.where(ln == rq, xnew, 0.0), axis=1, keepdims=True)

        knew = jnp.concatenate([u[:, WIDTH:2 * WIDTH], pad], axis=0).astype(_bf16)
        vnew = jnp.concatenate([u[:, 2 * WIDTH:3 * WIDTH], pad], axis=0).astype(_bf16)
        logit = jnp.where(ln <= rq, _scores_new(qa, knew) + (cq - xnew), NEG_BIG)
        m = jnp.max(logit, axis=1, keepdims=True)
        p = jnp.exp(logit - m)
        m_ref[...] = m
        l_ref[...] = jnp.sum(p, axis=1, keepdims=True)
        acca_ref[...] = _values_new(p.astype(_bf16), vnew, tq)

        knew = jnp.concatenate([u[:, 5 * WIDTH:6 * WIDTH], pad], axis=0).astype(_bf16)
        vnew = jnp.concatenate([u[:, 6 * WIDTH:7 * WIDTH], pad], axis=0).astype(_bf16)
        z = _scores_new(qb, knew)
        msk = ln < rq
        sp = jnp.where(msk, _softplus(z), 0.0)
        r = lax.broadcasted_iota(jnp.int32, (2 * LANES, LANES), 0)
        c = lax.broadcasted_iota(jnp.int32, (2 * LANES, LANES), 1)
        r = jnp.where(r >= LANES, r - LANES, r)
        suffix = jnp.where(r > c, 1.0, 0.0).astype(_bf16)
        within = jnp.dot(_split2_cols(sp), suffix, preferred_element_type=_f32)
        a = jnp.where(msk, jnp.exp(z - sp - within), 0.0)
        accb_ref[...] = _values_new(a.astype(_bf16), vnew, tq)
        later_ref[...] = jnp.sum(sp, axis=1, keepdims=True)

    xq = xnew_ref[...]
    cq = jnp.sum(jnp.where(ln == rq, xq, 0.0), axis=1, keepdims=True)
    ck = _expand_heads(cc_ref[i][0:N_HEADS, :], tq)
    s = _scores_cached(qa, ka_ref)
    z = _scores_cached(qb, kb_ref)
    logit = s + (cq - ck)
    m_old = m_ref[...]
    m_new = jnp.maximum(m_old, jnp.max(logit, axis=1, keepdims=True))
    alpha = jnp.exp(m_old - m_new)
    sp = _softplus(z)
    within = jnp.dot(_split2_cols(sp), tri_ref[...], preferred_element_type=_f32)
    p = jnp.exp(logit - m_new)
    m_ref[...] = m_new
    l_ref[...] = alpha * l_ref[...] + jnp.sum(p, axis=1, keepdims=True)
    acca_ref[...] = alpha * acca_ref[...] + _values_cached(p.astype(_bf16), va_ref, tq)
    a = jnp.exp(z - sp - within - later_ref[...])
    accb_ref[...] = accb_ref[...] + _values_cached(a.astype(_bf16), vb_ref, tq)
    later_ref[...] = later_ref[...] + jnp.sum(sp, axis=1, keepdims=True)

    @pl.when(i == nch - 1)
    def _():
        oa_ref[0] = _heads_to_lanes(acca_ref[...] / l_ref[...], tq)
        ob_ref[0] = _heads_to_lanes(accb_ref[...], tq)


def _sample_attention(layer, u, lf_new, lf_cache_t, ck_a, cv_a, ck_b, cv_b, tri):
    nb, tq, _ = u.shape
    past = ck_a.shape[4]
    kc = SAMPLE_CHUNK
    nch = past // kc
    rows = N_HEADS * tq
    chunk = (1, 1, N_HEADS, HEAD_DIM, kc)
    fwd = lambda bi, i: (layer, bi, 0, 0, i)
    rev = lambda bi, i: (layer, bi, 0, 0, nch - 1 - i)
    per_b = lambda bi, i: (bi, 0, 0)
    return pl.pallas_call(
        functools.partial(_sample_attn_kernel, tq=tq, nch=nch),
        out_shape=(jax.ShapeDtypeStruct((nb, tq, WIDTH), _f32),
                   jax.ShapeDtypeStruct((nb, tq, WIDTH), _f32)),
        grid=(nb, nch),
        in_specs=[
            pl.BlockSpec((1, tq, u.shape[2]), per_b),
            pl.BlockSpec((1, N_FORGET, LANES), per_b),
            pl.BlockSpec((1, N_FORGET, past), per_b),
            pl.BlockSpec(chunk, fwd),
            pl.BlockSpec(chunk, fwd),
            pl.BlockSpec(chunk, rev),
            pl.BlockSpec(chunk, rev),
            pl.BlockSpec(tri.shape, lambda bi, i: (0, 0)),
        ],
        out_specs=(pl.BlockSpec((1, tq, WIDTH), per_b), pl.BlockSpec((1, tq, WIDTH), per_b)),
        scratch_shapes=[
            pltpu.VMEM((nch, 16, kc), _f32),
            pltpu.VMEM((rows, LANES), _f32),
            pltpu.VMEM((rows, 1), _f32),
            pltpu.VMEM((rows, 1), _f32),
            pltpu.VMEM((rows, HEAD_DIM), _f32),
            pltpu.VMEM((rows, 1), _f32),
            pltpu.VMEM((rows, HEAD_DIM), _f32),
        ],
        compiler_params=pltpu.CompilerParams(
            dimension_semantics=("arbitrary", "arbitrary"), vmem_limit_bytes=VMEM_LIMIT),
        name="sample_attention",
    )(u, lf_new, lf_cache_t, ck_a, cv_a, ck_b, cv_b, tri)


def _split_weights(w):
    f0 = 3 * WIDTH
    w_main = jnp.concatenate([w[:, 0:f0], w[:, f0 + N_FORGET:]], axis=1).astype(_bf16)
    w_f = jnp.pad(w[:, f0:f0 + N_FORGET], ((0, 0), (0, LANES - N_FORGET))).astype(_bf16)
    return w_main, w_f


def kernel(x_prompt, x_sample, cache_fox_k, cache_fox_v, cache_fox_logf, cache_sb_k, cache_sb_v,
           norm_g, w_in, b_f, out_norm_a, out_norm_b, w_out, final_norm_g):
    depth = norm_g.shape[0]
    b, t, d = x_prompt.shape
    nb, tq, _ = x_sample.shape
    past = cache_fox_k.shape[2]
    nkb = t // KEY_BLOCK
    assert t % QUERY_BLOCK == 0 and t % PROJ_ROWS == 0 and past % SAMPLE_CHUNK == 0
    assert tq <= 16 and (nb * tq) % 8 == 0

    kc = SAMPLE_CHUNK
    r = lax.broadcasted_iota(jnp.int32, (2 * kc, kc), 0) % kc
    c = lax.broadcasted_iota(jnp.int32, (2 * kc, kc), 1)
    tri = jnp.where(r > c, 1.0, 0.0).astype(_bf16)
    fg = final_norm_g.reshape(1, d)
    time_minor = lambda a: a.transpose(0, 1, 3, 4, 2)
    cache_fox_k, cache_fox_v = time_minor(cache_fox_k), time_minor(cache_fox_v)
    cache_sb_k, cache_sb_v = time_minor(cache_sb_k), time_minor(cache_sb_v)

    xp, xs = x_prompt, x_sample.reshape(1, nb * tq, d)
    stacked, sample_new = None, []
    for l in range(depth):
        w_main, w_f = _split_weights(w_in[l])
        g = norm_g[l].reshape(1, d)
        bf = jnp.pad(b_f[l], (0, LANES - N_FORGET)).reshape(1, LANES)
        ga, gb = out_norm_a[l].reshape(1, WIDTH), out_norm_b[l].reshape(1, WIDTH)
        wo = w_out[l].astype(_bf16)
        final = l == depth - 1

        outs = _inproj(l, depth, stacked, xp, g, w_main, w_f, bf)
        stacked, (ct, qt, kbf, vt, z) = outs[:N_STACKED], outs[N_STACKED:]
        ct = ct.reshape(b, N_HEADS, nkb, KEY_BLOCK)
        xp = _merge(_attention(qt, kbf, vt, ct), z, xp, ga, gb, wo, fg, final)

        u, logf_s, logft_s = _sample_inproj(xs[0], g, w_main, w_f, bf)
        u3 = u.reshape(nb, tq, u.shape[1])
        lf_new = jnp.pad(logft_s.reshape(N_FORGET, nb, tq).transpose(1, 0, 2),
                         ((0, 0), (0, 0), (0, LANES - tq)))
        lf_cache_t = cache_fox_logf[l].transpose(0, 2, 1)
        oa_s, ob_s = _sample_attention(l, u3, lf_new, lf_cache_t,
                                       cache_fox_k, cache_fox_v, cache_sb_k, cache_sb_v, tri)
        z_s = jnp.concatenate([u[:, 3 * WIDTH:4 * WIDTH], u[:, 7 * WIDTH:8 * WIDTH]], axis=1)
        o_s = jnp.concatenate([oa_s, ob_s], axis=-1)
        xs = _merge(o_s.reshape(1, nb * tq, 2 * WIDTH), z_s.reshape(1, nb * tq, 2 * WIDTH),
                    xs, ga, gb, wo, fg, final)
        hs = lambda a: a.reshape(nb, tq, N_HEADS, HEAD_DIM)
        sample_new.append((hs(u[:, WIDTH:2 * WIDTH]), hs(u[:, 2 * WIDTH:3 * WIDTH]),
                           logf_s.reshape(nb, tq, N_FORGET),
                           hs(u[:, 5 * WIDTH:6 * WIDTH]), hs(u[:, 6 * WIDTH:7 * WIDTH])))

    heads = lambda a: a.reshape(depth, b, N_HEADS, HEAD_DIM, t).transpose(0, 1, 4, 2, 3)
    kat, vat, kbt, vbt, lft = stacked
    stack = lambda items, i: jnp.stack([it[i] for it in items])
    return (xp, xs.reshape(nb, tq, d),
            heads(kat), heads(vat), lft.transpose(0, 1, 3, 2), heads(kbt), heads(vbt),
            *(stack(sample_new, i) for i in range(5)))
```

```python
import functools

import jax
import jax.numpy as jnp
from jax import lax
from jax.experimental import pallas as pl
from jax.experimental.pallas import tpu as pltpu

HEAD_DIM = 64
N_HEADS = 8
WIDTH = N_HEADS * HEAD_DIM
N_FORGET = N_HEADS
EPS = 1e-6
SCALE = HEAD_DIM ** -0.5
LOG2E = 1.4426950408889634
NEG_BIG = -1e30
UNDERFLOW_EXP2 = 150.0

LANES = 128
KEY_BLOCK = 256
QUERY_BLOCK = 256
SLAB = 256
HEADS_PER_SLAB = SLAB // HEAD_DIM
PROJ_ROWS = 256
MERGE_ROWS = 1024
SAMPLE_CHUNK = 512
VMEM_LIMIT = 48 * 1024 * 1024

_f32 = jnp.float32
_bf16 = jnp.bfloat16


def _softplus(z):
    neg_abs = lax.bitcast_convert_type(
        lax.bitcast_convert_type(z, jnp.uint32) | jnp.uint32(0x80000000), _f32)
    return jnp.maximum(z, 0.0) + jnp.log(1.0 + jnp.exp(neg_abs))


def _softplus2(zs):
    neg_abs = lax.bitcast_convert_type(
        lax.bitcast_convert_type(zs, jnp.uint32) | jnp.uint32(0x80000000), _f32)
    return jnp.maximum(zs, 0.0) + jnp.log(1.0 + jnp.exp2(neg_abs)) * LOG2E


def _log_sigmoid(x):
    return jnp.minimum(x, 0.0) - jnp.log(1.0 + jnp.exp(-jnp.abs(x)))


def _rmsnorm(x, g):
    return x * lax.rsqrt(jnp.mean(x * x, axis=-1, keepdims=True) + EPS) * g


def _split3(x):
    a1 = x.astype(_bf16)
    r1 = x - a1.astype(_f32)
    a2 = r1.astype(_bf16)
    a3 = (r1 - a2.astype(_f32)).astype(_bf16)
    return a1, a2, a3


def _split2_cols(x):
    hi = lax.bitcast_convert_type(
        lax.bitcast_convert_type(x, jnp.uint32) & jnp.uint32(0xFFFF0000), _f32)
    return jnp.concatenate([hi.astype(_bf16), (x - hi).astype(_bf16)], axis=1)


def _lane_cumsum(x, carry):
    n = x.shape[1]
    r = lax.broadcasted_iota(jnp.int32, (n, n), 0)
    c = lax.broadcasted_iota(jnp.int32, (n, n), 1)
    tri = jnp.where(r <= c, 1.0, 0.0).astype(_bf16)
    a1, a2, a3 = _split3(x)
    dot = functools.partial(jnp.dot, preferred_element_type=_f32)
    return carry + (dot(a1, tri) + dot(a2, tri) + dot(a3, tri))


N_STACKED = 5


def _inproj_kernel(x_ref, g_ref, w_ref, wf_ref, bf_ref, *refs, first):
    if not first:
        refs = refs[N_STACKED:]
    kat_ref, vat_ref, kbt_ref, vbt_ref, lft_ref, ct_ref, qt_ref, kbf_ref, vt_ref, z_ref, carry_ref = refs
    tm = x_ref.shape[1]
    h = _rmsnorm(x_ref[0], g_ref[...]).astype(_bf16)

    def seg(i):
        return jnp.dot(h, w_ref[:, i * WIDTH:(i + 1) * WIDTH], preferred_element_type=_f32)

    def put_stacked(ref, value):
        for layer in range(ref.shape[0]):
            ref[layer, 0] = value

    for grp, (kt_ref, vtf_ref) in enumerate(((kat_ref, vat_ref), (kbt_ref, vbt_ref))):
        lo, hi = grp * WIDTH, (grp + 1) * WIDTH
        q = seg(4 * grp + 0) * (SCALE * LOG2E)
        qt_ref[0, lo:hi, :] = q.T.astype(_bf16)
        k = seg(4 * grp + 1)
        put_stacked(kt_ref, k.T)
        kbf_ref[0, :, lo:hi] = k.astype(_bf16)
        vt = seg(4 * grp + 2).T
        put_stacked(vtf_ref, vt)
        for s in range(tm // KEY_BLOCK):
            vt_ref[0, s, lo:hi, :] = vt[:, s * KEY_BLOCK:(s + 1) * KEY_BLOCK].astype(_bf16)
        z_ref[0, :, lo:hi] = seg(4 * grp + 3).astype(z_ref.dtype)

    fa = jnp.dot(h, wf_ref[...], preferred_element_type=_f32) + bf_ref[...]
    logft = _log_sigmoid(fa).T[0:16, :]
    put_stacked(lft_ref, logft[0:N_FORGET, :])

    @pl.when(pl.program_id(1) == 0)
    def _():
        carry_ref[...] = jnp.zeros_like(carry_ref)

    c = _lane_cumsum(logft, carry_ref[:, 0:1])
    ct_ref[0] = c[0:N_FORGET, :] * LOG2E
    carry_ref[...] = jnp.broadcast_to(c[:, tm - 1:tm], carry_ref.shape)


def _inproj(layer, depth, stacked, x, g, w_main, w_f, b_f):
    b, t, d = x.shape
    first = stacked is None
    tm = PROJ_ROWS if first else 2 * PROJ_ROWS
    nkb = t // KEY_BLOCK
    row = lambda bi, ti: (bi, ti, 0)
    const2 = lambda bi, ti: (0, 0)
    feat = lambda bi, ti: (bi, 0, ti)
    widths = (WIDTH, WIDTH, WIDTH, WIDTH, N_FORGET)
    slabs = depth if first else 1
    slab = lambda bi, ti: (0 if first else layer, bi, 0, ti)
    out_shape = tuple(jax.ShapeDtypeStruct((depth, b, w, t), _f32) for w in widths) + (
        jax.ShapeDtypeStruct((b, N_FORGET, t), _f32),
        jax.ShapeDtypeStruct((b, 2 * WIDTH, t), _bf16),
        jax.ShapeDtypeStruct((b, t, 2 * WIDTH), _bf16),
        jax.ShapeDtypeStruct((b, nkb, 2 * WIDTH, KEY_BLOCK), _bf16),
        jax.ShapeDtypeStruct((b, t, 2 * WIDTH), _bf16),
    )
    out_specs = tuple(pl.BlockSpec((slabs, 1, w, tm), slab) for w in widths) + (
        pl.BlockSpec((1, N_FORGET, tm), feat),
        pl.BlockSpec((1, 2 * WIDTH, tm), feat),
        pl.BlockSpec((1, tm, 2 * WIDTH), row),
        pl.BlockSpec((1, tm // KEY_BLOCK, 2 * WIDTH, KEY_BLOCK), lambda bi, ti: (bi, ti, 0, 0)),
        pl.BlockSpec((1, tm, 2 * WIDTH), row),
    )
    in_specs = [
        pl.BlockSpec((1, tm, d), row),
        pl.BlockSpec((1, d), const2),
        pl.BlockSpec(w_main.shape, const2),
        pl.BlockSpec(w_f.shape, const2),
        pl.BlockSpec((1, LANES), const2),
    ]
    args = [x, g, w_main, w_f, b_f]
    aliases = {}
    if not first:
        in_specs += [pl.BlockSpec(memory_space=pl.ANY)] * N_STACKED
        aliases = {len(args) + i: i for i in range(N_STACKED)}
        args += list(stacked)
    return pl.pallas_call(
        functools.partial(_inproj_kernel, first=first),
        out_shape=out_shape,
        grid=(b, t // tm),
        in_specs=in_specs,
        out_specs=out_specs,
        scratch_shapes=[pltpu.VMEM((16, LANES), _f32)],
        input_output_aliases=aliases,
        compiler_params=pltpu.CompilerParams(
            dimension_semantics=("arbitrary", "arbitrary"), vmem_limit_bytes=VMEM_LIMIT),
        name="prompt_inproj",
    )(*args)


def _head_queries(qt_ref):
    out = []
    for s in range(qt_ref.shape[1] // SLAB):
        qt = qt_ref[0, s * SLAB:(s + 1) * SLAB, :]
        head = lax.broadcasted_iota(jnp.int32, qt.shape, 0) // HEAD_DIM
        out += [jnp.where(head == hh, qt, jnp.zeros_like(qt)) for hh in range(SLAB // HEAD_DIM)]
    return out


_FOX_STAGES = (("scores", (), ("s",)), ("logits", ("s",), ("logit", "mnew", "alpha")),
               ("weights", ("logit", "mnew"), ("p",)), ("values", ("p", "alpha"), ()))
_SB_STAGES = (("scores", (), ("z",)), ("survival", ("z",), ("sp",)), ("suffix_sums", ("sp",), ("w",)),
              ("weights", ("z", "w"), ("a",)), ("values", ("a",), ()))


def _intermediate_type(name):
    tile = (KEY_BLOCK, QUERY_BLOCK)
    row = (1, QUERY_BLOCK)
    return {"s": (tile, _f32), "z": (tile, _f32), "logit": (tile, _f32), "p": (tile, _bf16), "sp": (tile, _bf16),
            "a": (tile, _bf16), "alpha": (row, _f32), "mnew": (row, _f32),
            "w": ((KEY_BLOCK + 16, QUERY_BLOCK), _f32)}[name]


class _Block:
    def __init__(self, kb, mask, handoff):
        self.kb, self.mask, self.handoff, self.vals = kb, mask, handoff, {}

    def put(self, name, h, value):
        self.vals[(name, h)] = value
        if (name, h) in self.handoff:
            self.handoff[(name, h)][...] = value

    def get(self, name, h):
        if (name, h) in self.vals:
            return self.vals[(name, h)]
        return self.handoff[(name, h)][...]


def _key_slab(k_ref, kb, h):
    s = h // HEADS_PER_SLAB
    return k_ref[0, pl.ds(pl.multiple_of(kb * KEY_BLOCK, KEY_BLOCK), KEY_BLOCK), s * SLAB:(s + 1) * SLAB]


_MIX_STEPS = 2 * N_HEADS
_MIX_ITEMS = ([(("sb", h), 2 * h, 2, _SB_STAGES) for h in range(N_HEADS)]
              + [(("fox", h), 2 * h + 1, 4, _FOX_STAGES) for h in range(N_HEADS)])


def _mix_handoff_keys():
    keys = []
    for (item, pos, gap, stages) in _MIX_ITEMS:
        done = min(len(stages), (_MIX_STEPS - 1 - pos) // gap + 1)
        made = {n for (_, _, w) in stages[:done] for n in w}
        need = {n for (_, r, _) in stages[done:] for n in r}
        keys += [(n, item) for n in sorted(made & need)]
    return keys


class _ItemPipeline:
    def __init__(self, fns):
        self.items = [(item, pos, gap, [fns[item[0]][name] for (name, _, _) in stages])
                      for (item, pos, gap, stages) in _MIX_ITEMS]
        self.n = _MIX_STEPS
        self.tail = max(pos + gap * (len(f) - 1) for (_, pos, gap, f) in self.items) + 1 - self.n
        assert 0 <= self.tail <= self.n

    def _step(self, blk, step, mixers):
        for (item, pos, gap, fns) in self.items:
            k, rem = divmod(step - pos, gap)
            if item[0] in mixers and step >= pos and rem == 0 and k < len(fns):
                fns[k](blk, item)

    def advance(self, prev, cur, prev_mixers=("fox", "sb"), cur_mixers=("fox", "sb")):
        for step in range(self.n):
            if prev is not None and step < self.tail:
                self._step(prev, self.n + step, prev_mixers)
            self._step(cur, step, cur_mixers)

    def finish(self, prev, mixers=("fox", "sb")):
        for step in range(self.tail):
            self._step(prev, self.n + step, mixers)


def _attention_kernel(qt_ref, k_ref, vt_ref, ct_ref, o_ref,
                      m_ref, accf_ref, later_ref, accs_ref, kmax_ref, bound_ref, *handoff_refs):
    j = pl.program_id(1)
    nh = N_HEADS
    handoff = dict(zip(_mix_handoff_keys(), handoff_refs))
    qh = _head_queries(qt_ref)
    nfull = j
    head = {"fox": lambda h: h, "sb": lambda h: nh + h}
    kio = lax.broadcasted_iota(jnp.int32, (KEY_BLOCK, QUERY_BLOCK), 0)
    qio = lax.broadcasted_iota(jnp.int32, (KEY_BLOCK, QUERY_BLOCK), 1)
    diag_mask = {"fox": kio <= qio, "sb": kio < qio}
    c0 = [ct_ref[0, h, pl.ds(nfull, 1), :][:, 0:1] for h in range(nh)]
    ones_rows = (lax.broadcasted_iota(jnp.int32, (16, KEY_BLOCK), 0) == 0).astype(_bf16)
    r = lax.broadcasted_iota(jnp.int32, (KEY_BLOCK + 16, KEY_BLOCK), 0)
    c = lax.broadcasted_iota(jnp.int32, (KEY_BLOCK + 16, KEY_BLOCK), 1)
    suffix = jnp.where((c >= r) | (r == KEY_BLOCK), 1.0, 0.0).astype(_bf16)

    m_ref[...] = jnp.full(m_ref.shape, NEG_BIG, _f32)
    accf_ref[...] = jnp.zeros(accf_ref.shape, _f32)
    later_ref[...] = jnp.zeros(later_ref.shape, _f32)
    accs_ref[...] = jnp.zeros(accs_ref.shape, _f32)

    def scores(name):
        def stage(blk, item):
            g = head[item[0]](item[1])
            s = jnp.dot(_key_slab(k_ref, blk.kb, g), qh[g], preferred_element_type=_f32)
            blk.put(name, item, s if blk.mask is None else jnp.where(blk.mask[item[0]], s, NEG_BIG))
        return stage

    def value_rows(blk, item):
        g = head[item[0]](item[1])
        return vt_ref[0, blk.kb, g * HEAD_DIM:(g + 1) * HEAD_DIM, :]

    def fox_logits(blk, item):
        h = item[1]
        ck_row = ct_ref[0, h, pl.ds(blk.kb, 1), :] - c0[h]
        ck = jnp.broadcast_to(ck_row, (LANES, KEY_BLOCK)).T
        ck = jnp.concatenate([ck] * (QUERY_BLOCK // LANES), axis=1)
        logit = blk.get("s", item) - ck
        m_old = m_ref[h]
        m_new = jnp.maximum(m_old, jnp.max(logit, axis=0, keepdims=True))
        blk.put("logit", item, logit)
        blk.put("mnew", item, m_new)
        blk.put("alpha", item, jnp.exp2(m_old - m_new))
        m_ref[h] = m_new

    def fox_weights(blk, item):
        blk.put("p", item, jnp.exp2(blk.get("logit", item) - blk.get("mnew", item)).astype(_bf16))

    def fox_values(blk, item):
        h = item[1]
        vth = jnp.concatenate([value_rows(blk, item), ones_rows], axis=0)
        accf_ref[h] = (blk.get("alpha", item) * accf_ref[h]
                       + jnp.dot(vth, blk.get("p", item), preferred_element_type=_f32))

    def sb_survival(blk, item):
        blk.put("sp", item, _softplus2(blk.get("z", item)).astype(_bf16))

    def sb_suffix_sums(blk, item):
        blk.put("w", item, jnp.dot(suffix, blk.get("sp", item), preferred_element_type=_f32))

    def sb_weights(blk, item):
        h = item[1]
        w = blk.get("w", item)
        later = later_ref[h]
        blk.put("a", item, jnp.exp2(blk.get("z", item) - w[0:KEY_BLOCK] - later).astype(_bf16))
        later_ref[h] = later + w[KEY_BLOCK:KEY_BLOCK + 1]

    def sb_values(blk, item):
        h = item[1]
        accs_ref[h] = accs_ref[h] + jnp.dot(value_rows(blk, item), blk.get("a", item),
                                            preferred_element_type=_f32)

    pipe = _ItemPipeline({
        "fox": {"scores": scores("s"), "logits": fox_logits, "weights": fox_weights, "values": fox_values},
        "sb": {"scores": scores("z"), "survival": sb_survival, "suffix_sums": sb_suffix_sums,
               "weights": sb_weights, "values": sb_values}})
    pipe.advance(None, _Block(nfull, diag_mask, handoff))

    sb_slab0 = nh // HEADS_PER_SLAB

    @pl.when(j == 0)
    def _():
        lane_head = lax.broadcasted_iota(jnp.int32, (SLAB, LANES), 0) // HEAD_DIM
        group = (lane_head == lax.broadcasted_iota(jnp.int32, (SLAB, LANES), 1)).astype(_bf16)
        for s in range(nh // HEADS_PER_SLAB):
            k = k_ref[0, :, (sb_slab0 + s) * SLAB:(sb_slab0 + s + 1) * SLAB].astype(_f32)
            norms2 = jnp.dot((k * k).astype(_bf16), group, preferred_element_type=_f32)
            kmax_ref[s] = jnp.max(norms2, axis=0, keepdims=True)
    lane = lax.broadcasted_iota(jnp.int32, (1, LANES), 1)
    qsb = qt_ref[0, nh * HEAD_DIM:2 * nh * HEAD_DIM, :]
    row_head = (lax.broadcasted_iota(jnp.int32, (16, qsb.shape[0]), 1) // HEAD_DIM
                == lax.broadcasted_iota(jnp.int32, (16, qsb.shape[0]), 0)).astype(_bf16)
    qn2 = jnp.dot(row_head, qsb * qsb, preferred_element_type=_f32)
    for h in range(nh):
        kmax2 = jnp.max(jnp.where(lane == h % HEADS_PER_SLAB, kmax_ref[h // HEADS_PER_SLAB], 0.0),
                        axis=1, keepdims=True)
        bound_ref[h] = 1.05 * jnp.sqrt(qn2[h:h + 1] * kmax2) + UNDERFLOW_EXP2

    weights_stage = [name for (name, _, _) in _SB_STAGES].index("weights")
    late_heads = [item[1] for (item, pos, gap, _) in _MIX_ITEMS
                  if item[0] == "sb" and pos + gap * weights_stage >= _MIX_STEPS]
    ones16 = jnp.ones((16, KEY_BLOCK), _bf16)

    def pending_mass(kb):
        mass, kmean = {}, {}
        for h in late_heads:
            g = nh + h
            s = g // HEADS_PER_SLAB
            if s not in kmean:
                ksum = jnp.dot(ones16, _key_slab(k_ref, kb, g), preferred_element_type=_f32)
                kmean[s] = (ksum * (1.0 / KEY_BLOCK)).astype(_bf16)
            zbar = jnp.dot(kmean[s], qh[g], preferred_element_type=_f32)[0:1]
            margin = (bound_ref[h] - UNDERFLOW_EXP2) * 2.0 ** -7
            mass[h] = (KEY_BLOCK * (1.0 - 2.0 ** -8)) * _softplus2(zbar - margin)
        return mass

    def unfinished(mass):
        slack = None
        for h in range(nh):
            gap_h = later_ref[h] - bound_ref[h]
            if h in mass:
                gap_h = gap_h + mass[h]
            slack = gap_h if slack is None else jnp.minimum(slack, gap_h)
        return jnp.min(slack) < 0.0

    def both(carry):
        i, _ = carry
        mass = pending_mass(nfull - 1 - i)
        pipe.advance(_Block(nfull - i, None, handoff), _Block(nfull - 1 - i, None, handoff))
        return i + 1, unfinished(mass)

    n_both, _ = lax.while_loop(lambda cr: jnp.logical_and(cr[0] < nfull, cr[1]), both,
                               (jnp.int32(0), unfinished({})))

    @pl.when(n_both == nfull)
    def _():
        pipe.finish(_Block(0, None, handoff))

    @pl.when(n_both < nfull)
    def _():
        pipe.advance(_Block(nfull - n_both, None, handoff), _Block(nfull - 1 - n_both, None, handoff),
                     cur_mixers=("fox",))

        @pl.loop(n_both + 1, nfull)
        def _(i):
            pipe.advance(_Block(nfull - i, None, handoff), _Block(nfull - 1 - i, None, handoff),
                         prev_mixers=("fox",), cur_mixers=("fox",))

        pipe.finish(_Block(0, None, handoff), mixers=("fox",))

    o = [accf_ref[h, 0:HEAD_DIM] / accf_ref[h, HEAD_DIM:HEAD_DIM + 1] for h in range(nh)]
    o += [accs_ref[h] for h in range(nh)]
    o_ref[0] = jnp.concatenate(o, axis=0).T.astype(o_ref.dtype)


def _attention(qt, kbf, vt, ct):
    b, _, t = qt.shape
    nkb = t // KEY_BLOCK
    assert QUERY_BLOCK == KEY_BLOCK
    scratch = [pltpu.VMEM((N_HEADS, 1, QUERY_BLOCK), _f32),
               pltpu.VMEM((N_HEADS, HEAD_DIM + 16, QUERY_BLOCK), _f32),
               pltpu.VMEM((N_HEADS, 1, QUERY_BLOCK), _f32),
               pltpu.VMEM((N_HEADS, HEAD_DIM, QUERY_BLOCK), _f32),
               pltpu.VMEM((N_HEADS // HEADS_PER_SLAB, 1, LANES), _f32),
               pltpu.VMEM((N_HEADS, 1, QUERY_BLOCK), _f32)]
    scratch += [pltpu.VMEM(*_intermediate_type(name)) for (name, _) in _mix_handoff_keys()]
    return pl.pallas_call(
        _attention_kernel,
        out_shape=jax.ShapeDtypeStruct((b, t, 2 * WIDTH), _bf16),
        grid=(b, t // QUERY_BLOCK),
        in_specs=[
            pl.BlockSpec((1, 2 * WIDTH, QUERY_BLOCK), lambda bi, j: (bi, 0, j)),
            pl.BlockSpec((1, t, 2 * WIDTH), lambda bi, j: (bi, 0, 0)),
            pl.BlockSpec((1, nkb, 2 * WIDTH, KEY_BLOCK), lambda bi, j: (bi, 0, 0, 0)),
            pl.BlockSpec((1, N_HEADS, nkb, KEY_BLOCK), lambda bi, j: (bi, 0, 0, 0)),
        ],
        out_specs=pl.BlockSpec((1, QUERY_BLOCK, 2 * WIDTH), lambda bi, j: (bi, j, 0)),
        scratch_shapes=scratch,
        compiler_params=pltpu.CompilerParams(
            dimension_semantics=("arbitrary", "arbitrary"), vmem_limit_bytes=VMEM_LIMIT),
        name="attention",
    )(qt, kbf, vt, ct)


def _merge_kernel(o_ref, z_ref, x_ref, ga_ref, gb_ref, w_ref, fg_ref, out_ref, *, final):
    def gated(o, g, z):
        return (_rmsnorm(o, g) * (z * (1.0 / (1.0 + jnp.exp(-z))))).astype(_bf16)

    z = z_ref[0].astype(_f32)
    o = o_ref[0].astype(_f32)
    ya = gated(o[:, 0:WIDTH], ga_ref[...], z[:, 0:WIDTH])
    yb = gated(o[:, WIDTH:2 * WIDTH], gb_ref[...], z[:, WIDTH:2 * WIDTH])
    y = (jnp.dot(ya, w_ref[0:WIDTH, :], preferred_element_type=_f32)
         + jnp.dot(yb, w_ref[WIDTH:2 * WIDTH, :], preferred_element_type=_f32))
    xn = x_ref[0] + y
    out_ref[0] = _rmsnorm(xn, fg_ref[...]) if final else xn


def _merge(o, z, x, ga, gb, w_out, fg, final):
    b, t, d = x.shape
    tm = min(MERGE_ROWS, t)
    row = lambda bi, ti: (bi, ti, 0)
    const2 = lambda bi, ti: (0, 0)
    return pl.pallas_call(
        functools.partial(_merge_kernel, final=final),
        out_shape=jax.ShapeDtypeStruct((b, t, d), _f32),
        grid=(b, t // tm),
        in_specs=[
            pl.BlockSpec((1, tm, 2 * WIDTH), row),
            pl.BlockSpec((1, tm, 2 * WIDTH), row),
            pl.BlockSpec((1, tm, d), row),
            pl.BlockSpec((1, WIDTH), const2),
            pl.BlockSpec((1, WIDTH), const2),
            pl.BlockSpec(w_out.shape, const2),
            pl.BlockSpec((1, d), const2),
        ],
        out_specs=pl.BlockSpec((1, tm, d), row),
        compiler_params=pltpu.CompilerParams(
            dimension_semantics=("arbitrary", "arbitrary"), vmem_limit_bytes=VMEM_LIMIT),
        name="merge_final" if final else "merge",
    )(o, z, x, ga, gb, w_out, fg)


def _sample_inproj_kernel(x_ref, g_ref, w_ref, wf_ref, bf_ref, u_ref, logf_ref, logft_ref):
    h = _rmsnorm(x_ref[...], g_ref[...]).astype(_bf16)
    u_ref[...] = jnp.dot(h, w_ref[...], preferred_element_type=_f32)
    fa = jnp.dot(h, wf_ref[...], preferred_element_type=_f32) + bf_ref[...]
    logf = _log_sigmoid(fa)
    logf_ref[...] = logf[:, 0:N_FORGET]
    logft_ref[...] = logf.T[0:N_FORGET, :]


def _sample_inproj(x2d, g, w_main, w_f, b_f):
    r, d = x2d.shape
    return pl.pallas_call(
        _sample_inproj_kernel,
        out_shape=(jax.ShapeDtypeStruct((r, w_main.shape[1]), _f32),
                   jax.ShapeDtypeStruct((r, N_FORGET), _f32),
                   jax.ShapeDtypeStruct((N_FORGET, r), _f32)),
        compiler_params=pltpu.CompilerParams(vmem_limit_bytes=VMEM_LIMIT),
        name="sample_inproj",
    )(x2d, g, w_main, w_f, b_f)


def _expand_heads(m, rows_per_head):
    return jnp.concatenate(
        [jnp.broadcast_to(m[h:h + 1, :], (rows_per_head, m.shape[1])) for h in range(N_HEADS)], axis=0)


def _nt_dot(a, b):
    return lax.dot_general(a, b, (((1,), (1,)), ((), ())), preferred_element_type=_f32)


def _head_cols(x, h):
    return x[:, h * HEAD_DIM:(h + 1) * HEAD_DIM]


def _scores_cached(q, kt_ref):
    return jnp.concatenate([jnp.dot(_head_cols(q, h), kt_ref[0, 0, h].astype(_bf16), preferred_element_type=_f32)
                            for h in range(N_HEADS)], axis=0)


def _values_cached(p, vt_ref, tq):
    return jnp.concatenate([_nt_dot(p[h * tq:(h + 1) * tq], vt_ref[0, 0, h].astype(_bf16))
                            for h in range(N_HEADS)], axis=0)


def _scores_new(q, knew):
    return jnp.concatenate([_nt_dot(_head_cols(q, h), _head_cols(knew, h)) for h in range(N_HEADS)], axis=0)


def _values_new(p, vnew, tq):
    return jnp.concatenate([jnp.dot(p[h * tq:(h + 1) * tq], _head_cols(vnew, h), preferred_element_type=_f32)
                            for h in range(N_HEADS)], axis=0)


def _heads_to_lanes(o, tq):
    return jnp.concatenate([o[h * tq:(h + 1) * tq] for h in range(N_HEADS)], axis=1)


def _sample_attn_kernel(u_ref, lfn_ref, lfc_ref, ka_ref, va_ref, kb_ref, vb_ref, tri_ref,
                        oa_ref, ob_ref,
                        cc_ref, xnew_ref, m_ref, l_ref, acca_ref, later_ref, accb_ref, *, tq, nch):
    i = pl.program_id(1)
    kc = SAMPLE_CHUNK
    rows = N_HEADS * tq
    u = u_ref[0]
    qa = (u[:, 0:WIDTH] * SCALE).astype(_bf16)
    qb = (u[:, 4 * WIDTH:5 * WIDTH] * SCALE).astype(_bf16)
    rq = lax.broadcasted_iota(jnp.int32, (rows, LANES), 0) % tq
    ln = lax.broadcasted_iota(jnp.int32, (rows, LANES), 1)
    pad = jnp.zeros((LANES - tq, WIDTH), _f32)

    @pl.when(i == 0)
    def _():
        carry = jnp.zeros((16, 1), _f32)
        zrows = jnp.zeros((16 - N_FORGET, kc), _f32)
        for ch in range(nch):
            x = jnp.concatenate([lfc_ref[0, :, ch * kc:(ch + 1) * kc], zrows], axis=0)
            cch = _lane_cumsum(x, carry)
            cc_ref[ch] = cch
            carry = cch[:, kc - 1:kc]
        xn = jnp.concatenate([lfn_ref[0], jnp.zeros((16 - N_FORGET, LANES), _f32)], axis=0)
        cnew = _lane_cumsum(xn, carry)
        xnew = _expand_heads(cnew[0:N_HEADS, :], tq)
        xnew_ref[...] = xnew
        cq = jnp.sum(jnp.where(ln == rq, xnew, 0.0), axis=1, keepdims=True)

        knew = jnp.concatenate([u[:, WIDTH:2 * WIDTH], pad], axis=0).astype(_bf16)
        vnew = jnp.concatenate([u[:, 2 * WIDTH:3 * WIDTH], pad], axis=0).astype(_bf16)
        logit = jnp.where(ln <= rq, _scores_new(qa, knew) + (cq - xnew), NEG_BIG)
        m = jnp.max(logit, axis=1, keepdims=True)
        p = jnp.exp(logit - m)
        m_ref[...] = m
        l_ref[...] = jnp.sum(p, axis=1, keepdims=True)
        acca_ref[...] = _values_new(p.astype(_bf16), vnew, tq)

        knew = jnp.concatenate([u[:, 5 * WIDTH:6 * WIDTH], pad], axis=0).astype(_bf16)
        vnew = jnp.concatenate([u[:, 6 * WIDTH:7 * WIDTH], pad], axis=0).astype(_bf16)
        z = _scores_new(qb, knew)
        msk = ln < rq
        sp = jnp.where(msk, _softplus(z), 0.0)
        r = lax.broadcasted_iota(jnp.int32, (2 * LANES, LANES), 0)
        c = lax.broadcasted_iota(jnp.int32, (2 * LANES, LANES), 1)
        r = jnp.where(r >= LANES, r - LANES, r)
        suffix = jnp.where(r > c, 1.0, 0.0).astype(_bf16)
        within = jnp.dot(_split2_cols(sp), suffix, preferred_element_type=_f32)
        a = jnp.where(msk, jnp.exp(z - sp - within), 0.0)
        accb_ref[...] = _values_new(a.astype(_bf16), vnew, tq)
        later_ref[...] = jnp.sum(sp, axis=1, keepdims=True)

    xq = xnew_ref[...]
    cq = jnp.sum(jnp.where(ln == rq, xq, 0.0), axis=1, keepdims=True)
    ck = _expand_heads(cc_ref[i][0:N_HEADS, :], tq)
    s = _scores_cached(qa, ka_ref)
    z = _scores_cached(qb, kb_ref)
    logit = s + (cq - ck)
    m_old = m_ref[...]
    m_new = jnp.maximum(m_old, jnp.max(logit, axis=1, keepdims=True))
    alpha = jnp.exp(m_old - m_new)
    sp = _softplus(z)
    within = jnp.dot(_split2_cols(sp), tri_ref[...], preferred_element_type=_f32)
    p = jnp.exp(logit - m_new)
    m_ref[...] = m_new
    l_ref[...] = alpha * l_ref[...] + jnp.sum(p, axis=1, keepdims=True)
    acca_ref[...] = alpha * acca_ref[...] + _values_cached(p.astype(_bf16), va_ref, tq)
    a = jnp.exp(z - sp - within - later_ref[...])
    accb_ref[...] = accb_ref[...] + _values_cached(a.astype(_bf16), vb_ref, tq)
    later_ref[...] = later_ref[...] + jnp.sum(sp, axis=1, keepdims=True)

    @pl.when(i == nch - 1)
    def _():
        oa_ref[0] = _heads_to_lanes(acca_ref[...] / l_ref[...], tq)
        ob_ref[0] = _heads_to_lanes(accb_ref[...], tq)


def _sample_attention(layer, u, lf_new, lf_cache_t, ck_a, cv_a, ck_b, cv_b, tri):
    nb, tq, _ = u.shape
    past = ck_a.shape[4]
    kc = SAMPLE_CHUNK
    nch = past // kc
    rows = N_HEADS * tq
    chunk = (1, 1, N_HEADS, HEAD_DIM, kc)
    fwd = lambda bi, i: (layer, bi, 0, 0, i)
    rev = lambda bi, i: (layer, bi, 0, 0, nch - 1 - i)
    per_b = lambda bi, i: (bi, 0, 0)
    return pl.pallas_call(
        functools.partial(_sample_attn_kernel, tq=tq, nch=nch),
        out_shape=(jax.ShapeDtypeStruct((nb, tq, WIDTH), _f32),
                   jax.ShapeDtypeStruct((nb, tq, WIDTH), _f32)),
        grid=(nb, nch),
        in_specs=[
            pl.BlockSpec((1, tq, u.shape[2]), per_b),
            pl.BlockSpec((1, N_FORGET, LANES), per_b),
            pl.BlockSpec((1, N_FORGET, past), per_b),
            pl.BlockSpec(chunk, fwd),
            pl.BlockSpec(chunk, fwd),
            pl.BlockSpec(chunk, rev),
            pl.BlockSpec(chunk, rev),
            pl.BlockSpec(tri.shape, lambda bi, i: (0, 0)),
        ],
        out_specs=(pl.BlockSpec((1, tq, WIDTH), per_b), pl.BlockSpec((1, tq, WIDTH), per_b)),
        scratch_shapes=[
            pltpu.VMEM((nch, 16, kc), _f32),
            pltpu.VMEM((rows, LANES), _f32),
            pltpu.VMEM((rows, 1), _f32),
            pltpu.VMEM((rows, 1), _f32),
            pltpu.VMEM((rows, HEAD_DIM), _f32),
            pltpu.VMEM((rows, 1), _f32),
            pltpu.VMEM((rows, HEAD_DIM), _f32),
        ],
        compiler_params=pltpu.CompilerParams(
            dimension_semantics=("arbitrary", "arbitrary"), vmem_limit_bytes=VMEM_LIMIT),
        name="sample_attention",
    )(u, lf_new, lf_cache_t, ck_a, cv_a, ck_b, cv_b, tri)


def _split_weights(w):
    f0 = 3 * WIDTH
    w_main = jnp.concatenate([w[:, 0:f0], w[:, f0 + N_FORGET:]], axis=1).astype(_bf16)
    w_f = jnp.pad(w[:, f0:f0 + N_FORGET], ((0, 0), (0, LANES - N_FORGET))).astype(_bf16)
    return w_main, w_f


def kernel(x_prompt, x_sample, cache_fox_k, cache_fox_v, cache_fox_logf, cache_sb_k, cache_sb_v,
           norm_g, w_in, b_f, out_norm_a, out_norm_b, w_out, final_norm_g):
    depth = norm_g.shape[0]
    b, t, d = x_prompt.shape
    nb, tq, _ = x_sample.shape
    past = cache_fox_k.shape[2]
    nkb = t // KEY_BLOCK
    assert t % QUERY_BLOCK == 0 and t % PROJ_ROWS == 0 and past % SAMPLE_CHUNK == 0
    assert tq <= 16 and (nb * tq) % 8 == 0

    kc = SAMPLE_CHUNK
    r = lax.broadcasted_iota(jnp.int32, (2 * kc, kc), 0) % kc
    c = lax.broadcasted_iota(jnp.int32, (2 * kc, kc), 1)
    tri = jnp.where(r > c, 1.0, 0.0).astype(_bf16)
    fg = final_norm_g.reshape(1, d)
    time_minor = lambda a: a.transpose(0, 1, 3, 4, 2)
    cache_fox_k, cache_fox_v = time_minor(cache_fox_k), time_minor(cache_fox_v)
    cache_sb_k, cache_sb_v = time_minor(cache_sb_k), time_minor(cache_sb_v)

    xp, xs = x_prompt, x_sample.reshape(1, nb * tq, d)
    stacked, sample_new = None, []
    for l in range(depth):
        w_main, w_f = _split_weights(w_in[l])
        g = norm_g[l].reshape(1, d)
        bf = jnp.pad(b_f[l], (0, LANES - N_FORGET)).reshape(1, LANES)
        ga, gb = out_norm_a[l].reshape(1, WIDTH), out_norm_b[l].reshape(1, WIDTH)
        wo = w_out[l].astype(_bf16)
        final = l == depth - 1

        outs = _inproj(l, depth, stacked, xp, g, w_main, w_f, bf)
        stacked, (ct, qt, kbf, vt, z) = outs[:N_STACKED], outs[N_STACKED:]
        ct = ct.reshape(b, N_HEADS, nkb, KEY_BLOCK)
        xp = _merge(_attention(qt, kbf, vt, ct), z, xp, ga, gb, wo, fg, final)

        u, logf_s, logft_s = _sample_inproj(xs[0], g, w_main, w_f, bf)
        u3 = u.reshape(nb, tq, u.shape[1])
        lf_new = jnp.pad(logft_s.reshape(N_FORGET, nb, tq).transpose(1, 0, 2),
                         ((0, 0), (0, 0), (0, LANES - tq)))
        lf_cache_t = cache_fox_logf[l].transpose(0, 2, 1)
        oa_s, ob_s = _sample_attention(l, u3, lf_new, lf_cache_t,
                                       cache_fox_k, cache_fox_v, cache_sb_k, cache_sb_v, tri)
        z_s = jnp.concatenate([u[:, 3 * WIDTH:4 * WIDTH], u[:, 7 * WIDTH:8 * WIDTH]], axis=1)
        o_s = jnp.concatenate([oa_s, ob_s], axis=-1)
        xs = _merge(o_s.reshape(1, nb * tq, 2 * WIDTH), z_s.reshape(1, nb * tq, 2 * WIDTH),
                    xs, ga, gb, wo, fg, final)
        hs = lambda a: a.reshape(nb, tq, N_HEADS, HEAD_DIM)
        sample_new.append((hs(u[:, WIDTH:2 * WIDTH]), hs(u[:, 2 * WIDTH:3 * WIDTH]),
                           logf_s.reshape(nb, tq, N_FORGET),
                           hs(u[:, 5 * WIDTH:6 * WIDTH]), hs(u[:, 6 * WIDTH:7 * WIDTH])))

    heads = lambda a: a.reshape(depth, b, N_HEADS, HEAD_DIM, t).transpose(0, 1, 4, 2, 3)
    kat, vat, kbt, vbt, lft = stacked
    stack = lambda items, i: jnp.stack([it[i] for it in items])
    return (xp, xs.reshape(nb, tq, d),
            heads(kat), heads(vat), lft.transpose(0, 1, 3, 2), heads(kbt), heads(vbt),
            *(stack(sample_new, i) for i in range(5)))
```

```python
import functools

import jax
import jax.numpy as jnp
from jax import lax
from jax.experimental import pallas as pl
from jax.experimental.pallas import tpu as pltpu

HEAD_DIM = 64
N_HEADS = 8
WIDTH = N_HEADS * HEAD_DIM
N_FORGET = N_HEADS
EPS = 1e-6
SCALE = HEAD_DIM ** -0.5
LOG2E = 1.4426950408889634
NEG_BIG = -1e30
UNDERFLOW_EXP2 = 150.0

LANES = 128
KEY_BLOCK = 256
QUERY_BLOCK = 256
SLAB = 256
HEADS_PER_SLAB = SLAB // HEAD_DIM
PROJ_ROWS = 256
MERGE_ROWS = 1024
SAMPLE_CHUNK = 512
VMEM_LIMIT = 48 * 1024 * 1024

_f32 = jnp.float32
_bf16 = jnp.bfloat16


def _softplus(z):
    neg_abs = lax.bitcast_convert_type(
        lax.bitcast_convert_type(z, jnp.uint32) | jnp.uint32(0x80000000), _f32)
    return jnp.maximum(z, 0.0) + jnp.log(1.0 + jnp.exp(neg_abs))


def _softplus2(zs):
    neg_abs = lax.bitcast_convert_type(
        lax.bitcast_convert_type(zs, jnp.uint32) | jnp.uint32(0x80000000), _f32)
    return jnp.maximum(zs, 0.0) + jnp.log(1.0 + jnp.exp2(neg_abs)) * LOG2E


def _log_sigmoid(x):
    return jnp.minimum(x, 0.0) - jnp.log(1.0 + jnp.exp(-jnp.abs(x)))


def _rmsnorm(x, g):
    return x * lax.rsqrt(jnp.mean(x * x, axis=-1, keepdims=True) + EPS) * g


def _split3(x):
    a1 = x.astype(_bf16)
    r1 = x - a1.astype(_f32)
    a2 = r1.astype(_bf16)
    a3 = (r1 - a2.astype(_f32)).astype(_bf16)
    return a1, a2, a3


def _split2_cols(x):
    hi = lax.bitcast_convert_type(
        lax.bitcast_convert_type(x, jnp.uint32) & jnp.uint32(0xFFFF0000), _f32)
    return jnp.concatenate([hi.astype(_bf16), (x - hi).astype(_bf16)], axis=1)


def _lane_cumsum(x, carry):
    n = x.shape[1]
    r = lax.broadcasted_iota(jnp.int32, (n, n), 0)
    c = lax.broadcasted_iota(jnp.int32, (n, n), 1)
    tri = jnp.where(r <= c, 1.0, 0.0).astype(_bf16)
    a1, a2, a3 = _split3(x)
    dot = functools.partial(jnp.dot, preferred_element_type=_f32)
    return carry + (dot(a1, tri) + dot(a2, tri) + dot(a3, tri))


N_STACKED = 5


def _inproj_kernel(x_ref, g_ref, w_ref, wf_ref, bf_ref, *refs, first):
    if not first:
        refs = refs[N_STACKED:]
    kat_ref, vat_ref, kbt_ref, vbt_ref, lft_ref, ct_ref, qt_ref, kbf_ref, vt_ref, z_ref, carry_ref = refs
    tm = x_ref.shape[1]
    h = _rmsnorm(x_ref[0], g_ref[...]).astype(_bf16)

    def seg(i):
        return jnp.dot(h, w_ref[:, i * WIDTH:(i + 1) * WIDTH], preferred_element_type=_f32)

    def put_stacked(ref, value):
        for layer in range(ref.shape[0]):
            ref[layer, 0] = value

    for grp, (kt_ref, vtf_ref) in enumerate(((kat_ref, vat_ref), (kbt_ref, vbt_ref))):
        lo, hi = grp * WIDTH, (grp + 1) * WIDTH
        q = seg(4 * grp + 0) * (SCALE * LOG2E)
        qt_ref[0, lo:hi, :] = q.T.astype(_bf16)
        k = seg(4 * grp + 1)
        put_stacked(kt_ref, k.T)
        kbf_ref[0, :, lo:hi] = k.astype(_bf16)
        vt = seg(4 * grp + 2).T
        put_stacked(vtf_ref, vt)
        for s in range(tm // KEY_BLOCK):
            vt_ref[0, s, lo:hi, :] = vt[:, s * KEY_BLOCK:(s + 1) * KEY_BLOCK].astype(_bf16)
        z_ref[0, :, lo:hi] = seg(4 * grp + 3).astype(z_ref.dtype)

    fa = jnp.dot(h, wf_ref[...], preferred_element_type=_f32) + bf_ref[...]
    logft = _log_sigmoid(fa).T[0:16, :]
    put_stacked(lft_ref, logft[0:N_FORGET, :])

    @pl.when(pl.program_id(1) == 0)
    def _():
        carry_ref[...] = jnp.zeros_like(carry_ref)

    c = _lane_cumsum(logft, carry_ref[:, 0:1])
    ct_ref[0] = c[0:N_FORGET, :] * LOG2E
    carry_ref[...] = jnp.broadcast_to(c[:, tm - 1:tm], carry_ref.shape)


def _inproj(layer, depth, stacked, x, g, w_main, w_f, b_f):
    b, t, d = x.shape
    first = stacked is None
    tm = 2 * PROJ_ROWS
    nkb = t // KEY_BLOCK
    row = lambda bi, ti: (bi, ti, 0)
    const2 = lambda bi, ti: (0, 0)
    feat = lambda bi, ti: (bi, 0, ti)
    widths = (WIDTH, WIDTH, WIDTH, WIDTH, N_FORGET)
    slabs = depth if first else 1
    slab = lambda bi, ti: (0 if first else layer, bi, 0, ti)
    out_shape = tuple(jax.ShapeDtypeStruct((depth, b, w, t), _f32) for w in widths) + (
        jax.ShapeDtypeStruct((b, N_FORGET, t), _f32),
        jax.ShapeDtypeStruct((b, 2 * WIDTH, t), _bf16),
        jax.ShapeDtypeStruct((b, t, 2 * WIDTH), _bf16),
        jax.ShapeDtypeStruct((b, nkb, 2 * WIDTH, KEY_BLOCK), _bf16),
        jax.ShapeDtypeStruct((b, t, 2 * WIDTH), _bf16),
    )
    out_specs = tuple(pl.BlockSpec((slabs, 1, w, tm), slab) for w in widths) + (
        pl.BlockSpec((1, N_FORGET, tm), feat),
        pl.BlockSpec((1, 2 * WIDTH, tm), feat),
        pl.BlockSpec((1, tm, 2 * WIDTH), row),
        pl.BlockSpec((1, tm // KEY_BLOCK, 2 * WIDTH, KEY_BLOCK), lambda bi, ti: (bi, ti, 0, 0)),
        pl.BlockSpec((1, tm, 2 * WIDTH), row),
    )
    in_specs = [
        pl.BlockSpec((1, tm, d), row),
        pl.BlockSpec((1, d), const2),
        pl.BlockSpec(w_main.shape, const2, pipeline_mode=pl.Buffered(1)),
        pl.BlockSpec(w_f.shape, const2, pipeline_mode=pl.Buffered(1)),
        pl.BlockSpec((1, LANES), const2),
    ]
    args = [x, g, w_main, w_f, b_f]
    aliases = {}
    if not first:
        in_specs += [pl.BlockSpec(memory_space=pl.ANY)] * N_STACKED
        aliases = {len(args) + i: i for i in range(N_STACKED)}
        args += list(stacked)
    return pl.pallas_call(
        functools.partial(_inproj_kernel, first=first),
        out_shape=out_shape,
        grid=(b, t // tm),
        in_specs=in_specs,
        out_specs=out_specs,
        scratch_shapes=[pltpu.VMEM((16, LANES), _f32)],
        input_output_aliases=aliases,
        compiler_params=pltpu.CompilerParams(
            dimension_semantics=("arbitrary", "arbitrary"), vmem_limit_bytes=VMEM_LIMIT),
        name="prompt_inproj",
    )(*args)


def _head_queries(qt_ref):
    out = []
    for s in range(qt_ref.shape[1] // SLAB):
        qt = qt_ref[0, s * SLAB:(s + 1) * SLAB, :]
        head = lax.broadcasted_iota(jnp.int32, qt.shape, 0) // HEAD_DIM
        out += [jnp.where(head == hh, qt, jnp.zeros_like(qt)) for hh in range(SLAB // HEAD_DIM)]
    return out


_FOX_STAGES = (("scores", (), ("s",)), ("logits", ("s",), ("logit", "mnew", "alpha")),
               ("weights", ("logit", "mnew"), ("p",)), ("values", ("p", "alpha"), ()))
_SB_STAGES = (("scores", (), ("z",)), ("survival", ("z",), ("sp",)), ("suffix_sums", ("sp",), ("w",)),
              ("weights", ("z", "w"), ("a",)), ("values", ("a",), ()))


def _intermediate_type(name):
    tile = (KEY_BLOCK, QUERY_BLOCK)
    row = (1, QUERY_BLOCK)
    return {"s": (tile, _f32), "z": (tile, _f32), "logit": (tile, _f32), "p": (tile, _bf16), "sp": (tile, _bf16),
            "a": (tile, _bf16), "alpha": (row, _f32), "mnew": (row, _f32),
            "w": ((KEY_BLOCK + 16, QUERY_BLOCK), _f32)}[name]


class _Block:
    def __init__(self, kb, mask, handoff):
        self.kb, self.mask, self.handoff, self.vals = kb, mask, handoff, {}

    def put(self, name, h, value):
        self.vals[(name, h)] = value
        if (name, h) in self.handoff:
            self.handoff[(name, h)][...] = value

    def get(self, name, h):
        if (name, h) in self.vals:
            return self.vals[(name, h)]
        return self.handoff[(name, h)][...]


def _key_slab(k_ref, kb, h):
    s = h // HEADS_PER_SLAB
    return k_ref[0, pl.ds(pl.multiple_of(kb * KEY_BLOCK, KEY_BLOCK), KEY_BLOCK), s * SLAB:(s + 1) * SLAB]


_MIX_STEPS = 2 * N_HEADS
_MIX_ITEMS = ([(("sb", h), 2 * h, 2, _SB_STAGES) for h in range(N_HEADS)]
              + [(("fox", h), 2 * h + 1, 4, _FOX_STAGES) for h in range(N_HEADS)])


def _mix_handoff_keys():
    keys = []
    for (item, pos, gap, stages) in _MIX_ITEMS:
        done = min(len(stages), (_MIX_STEPS - 1 - pos) // gap + 1)
        made = {n for (_, _, w) in stages[:done] for n in w}
        need = {n for (_, r, _) in stages[done:] for n in r}
        keys += [(n, item) for n in sorted(made & need)]
    return keys


class _ItemPipeline:
    def __init__(self, fns):
        self.items = [(item, pos, gap, [fns[item[0]][name] for (name, _, _) in stages])
                      for (item, pos, gap, stages) in _MIX_ITEMS]
        self.n = _MIX_STEPS
        self.tail = max(pos + gap * (len(f) - 1) for (_, pos, gap, f) in self.items) + 1 - self.n
        assert 0 <= self.tail <= self.n

    def _step(self, blk, step, mixers):
        for (item, pos, gap, fns) in self.items:
            k, rem = divmod(step - pos, gap)
            if item[0] in mixers and step >= pos and rem == 0 and k < len(fns):
                fns[k](blk, item)

    def advance(self, prev, cur, prev_mixers=("fox", "sb"), cur_mixers=("fox", "sb")):
        for step in range(self.n):
            if prev is not None and step < self.tail:
                self._step(prev, self.n + step, prev_mixers)
            self._step(cur, step, cur_mixers)

    def finish(self, prev, mixers=("fox", "sb")):
        for step in range(self.tail):
            self._step(prev, self.n + step, mixers)


def _attention_kernel(qt_ref, k_ref, vt_ref, ct_ref, o_ref,
                      m_ref, accf_ref, later_ref, accs_ref, kmax_ref, bound_ref, *handoff_refs):
    j = pl.program_id(1)
    nh = N_HEADS
    handoff = dict(zip(_mix_handoff_keys(), handoff_refs))
    qh = _head_queries(qt_ref)
    nfull = j
    head = {"fox": lambda h: h, "sb": lambda h: nh + h}
    kio = lax.broadcasted_iota(jnp.int32, (KEY_BLOCK, QUERY_BLOCK), 0)
    qio = lax.broadcasted_iota(jnp.int32, (KEY_BLOCK, QUERY_BLOCK), 1)
    diag_mask = {"fox": kio <= qio, "sb": kio < qio}
    c0 = [ct_ref[0, h, pl.ds(nfull, 1), :][:, 0:1] for h in range(nh)]
    ones_rows = (lax.broadcasted_iota(jnp.int32, (16, KEY_BLOCK), 0) == 0).astype(_bf16)
    r = lax.broadcasted_iota(jnp.int32, (KEY_BLOCK + 16, KEY_BLOCK), 0)
    c = lax.broadcasted_iota(jnp.int32, (KEY_BLOCK + 16, KEY_BLOCK), 1)
    suffix = jnp.where((c >= r) | (r == KEY_BLOCK), 1.0, 0.0).astype(_bf16)

    m_ref[...] = jnp.full(m_ref.shape, NEG_BIG, _f32)
    accf_ref[...] = jnp.zeros(accf_ref.shape, _f32)
    later_ref[...] = jnp.zeros(later_ref.shape, _f32)
    accs_ref[...] = jnp.zeros(accs_ref.shape, _f32)

    def scores(name):
        def stage(blk, item):
            g = head[item[0]](item[1])
            s = jnp.dot(_key_slab(k_ref, blk.kb, g), qh[g], preferred_element_type=_f32)
            blk.put(name, item, s if blk.mask is None else jnp.where(blk.mask[item[0]], s, NEG_BIG))
        return stage

    def value_rows(blk, item):
        g = head[item[0]](item[1])
        return vt_ref[0, blk.kb, g * HEAD_DIM:(g + 1) * HEAD_DIM, :]

    def fox_logits(blk, item):
        h = item[1]
        ck_row = ct_ref[0, h, pl.ds(blk.kb, 1), :] - c0[h]
        ck = jnp.broadcast_to(ck_row, (LANES, KEY_BLOCK)).T
        ck = jnp.concatenate([ck] * (QUERY_BLOCK // LANES), axis=1)
        logit = blk.get("s", item) - ck
        m_old = m_ref[h]
        m_new = jnp.maximum(m_old, jnp.max(logit, axis=0, keepdims=True))
        blk.put("logit", item, logit)
        blk.put("mnew", item, m_new)
        blk.put("alpha", item, jnp.exp2(m_old - m_new))
        m_ref[h] = m_new

    def fox_weights(blk, item):
        blk.put("p", item, jnp.exp2(blk.get("logit", item) - blk.get("mnew", item)).astype(_bf16))

    def fox_values(blk, item):
        h = item[1]
        vth = jnp.concatenate([value_rows(blk, item), ones_rows], axis=0)
        accf_ref[h] = (blk.get("alpha", item) * accf_ref[h]
                       + jnp.dot(vth, blk.get("p", item), preferred_element_type=_f32))

    def sb_survival(blk, item):
        blk.put("sp", item, _softplus2(blk.get("z", item)).astype(_bf16))

    def sb_suffix_sums(blk, item):
        blk.put("w", item, jnp.dot(suffix, blk.get("sp", item), preferred_element_type=_f32))

    def sb_weights(blk, item):
        h = item[1]
        w = blk.get("w", item)
        later = later_ref[h]
        blk.put("a", item, jnp.exp2(blk.get("z", item) - w[0:KEY_BLOCK] - later).astype(_bf16))
        later_ref[h] = later + w[KEY_BLOCK:KEY_BLOCK + 1]

    def sb_values(blk, item):
        h = item[1]
        accs_ref[h] = accs_ref[h] + jnp.dot(value_rows(blk, item), blk.get("a", item),
                                            preferred_element_type=_f32)

    pipe = _ItemPipeline({
        "fox": {"scores": scores("s"), "logits": fox_logits, "weights": fox_weights, "values": fox_values},
        "sb": {"scores": scores("z"), "survival": sb_survival, "suffix_sums": sb_suffix_sums,
               "weights": sb_weights, "values": sb_values}})
    pipe.advance(None, _Block(nfull, diag_mask, handoff))

    sb_slab0 = nh // HEADS_PER_SLAB

    @pl.when(j == 0)
    def _():
        lane_head = lax.broadcasted_iota(jnp.int32, (SLAB, LANES), 0) // HEAD_DIM
        group = (lane_head == lax.broadcasted_iota(jnp.int32, (SLAB, LANES), 1)).astype(_bf16)
        for s in range(nh // HEADS_PER_SLAB):
            k = k_ref[0, :, (sb_slab0 + s) * SLAB:(sb_slab0 + s + 1) * SLAB].astype(_f32)
            norms2 = jnp.dot((k * k).astype(_bf16), group, preferred_element_type=_f32)
            kmax_ref[s] = jnp.max(norms2, axis=0, keepdims=True)
    lane = lax.broadcasted_iota(jnp.int32, (1, LANES), 1)
    qsb = qt_ref[0, nh * HEAD_DIM:2 * nh * HEAD_DIM, :]
    row_head = (lax.broadcasted_iota(jnp.int32, (16, qsb.shape[0]), 1) // HEAD_DIM
                == lax.broadcasted_iota(jnp.int32, (16, qsb.shape[0]), 0)).astype(_bf16)
    qn2 = jnp.dot(row_head, qsb * qsb, preferred_element_type=_f32)
    for h in range(nh):
        kmax2 = jnp.max(jnp.where(lane == h % HEADS_PER_SLAB, kmax_ref[h // HEADS_PER_SLAB], 0.0),
                        axis=1, keepdims=True)
        bound_ref[h] = 1.05 * jnp.sqrt(qn2[h:h + 1] * kmax2) + UNDERFLOW_EXP2

    weights_stage = [name for (name, _, _) in _SB_STAGES].index("weights")
    late_heads = [item[1] for (item, pos, gap, _) in _MIX_ITEMS
                  if item[0] == "sb" and pos + gap * weights_stage >= _MIX_STEPS]
    ones16 = jnp.ones((16, KEY_BLOCK), _bf16)

    def pending_mass(kb):
        mass, kmean = {}, {}
        for h in late_heads:
            g = nh + h
            s = g // HEADS_PER_SLAB
            if s not in kmean:
                ksum = jnp.dot(ones16, _key_slab(k_ref, kb, g), preferred_element_type=_f32)
                kmean[s] = (ksum * (1.0 / KEY_BLOCK)).astype(_bf16)
            zbar = jnp.dot(kmean[s], qh[g], preferred_element_type=_f32)[0:1]
            margin = (bound_ref[h] - UNDERFLOW_EXP2) * 2.0 ** -7
            mass[h] = (KEY_BLOCK * (1.0 - 2.0 ** -8)) * _softplus2(zbar - margin)
        return mass

    def unfinished(mass):
        slack = None
        for h in range(nh):
            gap_h = later_ref[h] - bound_ref[h]
            if h in mass:
                gap_h = gap_h + mass[h]
            slack = gap_h if slack is None else jnp.minimum(slack, gap_h)
        return jnp.min(slack) < 0.0

    def both(carry):
        i, _ = carry
        mass = pending_mass(nfull - 1 - i)
        pipe.advance(_Block(nfull - i, None, handoff), _Block(nfull - 1 - i, None, handoff))
        return i + 1, unfinished(mass)

    n_both, _ = lax.while_loop(lambda cr: jnp.logical_and(cr[0] < nfull, cr[1]), both,
                               (jnp.int32(0), unfinished({})))

    @pl.when(n_both == nfull)
    def _():
        pipe.finish(_Block(0, None, handoff))

    @pl.when(n_both < nfull)
    def _():
        pipe.advance(_Block(nfull - n_both, None, handoff), _Block(nfull - 1 - n_both, None, handoff),
                     cur_mixers=("fox",))

        @pl.loop(n_both + 1, nfull)
        def _(i):
            pipe.advance(_Block(nfull - i, None, handoff), _Block(nfull - 1 - i, None, handoff),
                         prev_mixers=("fox",), cur_mixers=("fox",))

        pipe.finish(_Block(0, None, handoff), mixers=("fox",))

    o = [accf_ref[h, 0:HEAD_DIM] / accf_ref[h, HEAD_DIM:HEAD_DIM + 1] for h in range(nh)]
    o += [accs_ref[h] for h in range(nh)]
    o_ref[0] = jnp.concatenate(o, axis=0).T.astype(o_ref.dtype)


def _attention(qt, kbf, vt, ct):
    b, _, t = qt.shape
    nkb = t // KEY_BLOCK
    assert QUERY_BLOCK == KEY_BLOCK
    scratch = [pltpu.VMEM((N_HEADS, 1, QUERY_BLOCK), _f32),
               pltpu.VMEM((N_HEADS, HEAD_DIM + 16, QUERY_BLOCK), _f32),
               pltpu.VMEM((N_HEADS, 1, QUERY_BLOCK), _f32),
               pltpu.VMEM((N_HEADS, HEAD_DIM, QUERY_BLOCK), _f32),
               pltpu.VMEM((N_HEADS // HEADS_PER_SLAB, 1, LANES), _f32),
               pltpu.VMEM((N_HEADS, 1, QUERY_BLOCK), _f32)]
    scratch += [pltpu.VMEM(*_intermediate_type(name)) for (name, _) in _mix_handoff_keys()]
    return pl.pallas_call(
        _attention_kernel,
        out_shape=jax.ShapeDtypeStruct((b, t, 2 * WIDTH), _bf16),
        grid=(b, t // QUERY_BLOCK),
        in_specs=[
            pl.BlockSpec((1, 2 * WIDTH, QUERY_BLOCK), lambda bi, j: (bi, 0, j)),
            pl.BlockSpec((1, t, 2 * WIDTH), lambda bi, j: (bi, 0, 0)),
            pl.BlockSpec((1, nkb, 2 * WIDTH, KEY_BLOCK), lambda bi, j: (bi, 0, 0, 0)),
            pl.BlockSpec((1, N_HEADS, nkb, KEY_BLOCK), lambda bi, j: (bi, 0, 0, 0)),
        ],
        out_specs=pl.BlockSpec((1, QUERY_BLOCK, 2 * WIDTH), lambda bi, j: (bi, j, 0)),
        scratch_shapes=scratch,
        compiler_params=pltpu.CompilerParams(
            dimension_semantics=("arbitrary", "arbitrary"), vmem_limit_bytes=VMEM_LIMIT),
        name="attention",
    )(qt, kbf, vt, ct)


def _merge_kernel(o_ref, z_ref, x_ref, ga_ref, gb_ref, w_ref, fg_ref, out_ref, *, final):
    def gated(o, g, z):
        return (_rmsnorm(o, g) * (z * (1.0 / (1.0 + jnp.exp(-z))))).astype(_bf16)

    z = z_ref[0].astype(_f32)
    o = o_ref[0].astype(_f32)
    ya = gated(o[:, 0:WIDTH], ga_ref[...], z[:, 0:WIDTH])
    yb = gated(o[:, WIDTH:2 * WIDTH], gb_ref[...], z[:, WIDTH:2 * WIDTH])
    y = (jnp.dot(ya, w_ref[0:WIDTH, :], preferred_element_type=_f32)
         + jnp.dot(yb, w_ref[WIDTH:2 * WIDTH, :], preferred_element_type=_f32))
    xn = x_ref[0] + y
    out_ref[0] = _rmsnorm(xn, fg_ref[...]) if final else xn


def _merge(o, z, x, ga, gb, w_out, fg, final):
    b, t, d = x.shape
    tm = min(MERGE_ROWS, t)
    row = lambda bi, ti: (bi, ti, 0)
    const2 = lambda bi, ti: (0, 0)
    return pl.pallas_call(
        functools.partial(_merge_kernel, final=final),
        out_shape=jax.ShapeDtypeStruct((b, t, d), _f32),
        grid=(b, t // tm),
        in_specs=[
            pl.BlockSpec((1, tm, 2 * WIDTH), row),
            pl.BlockSpec((1, tm, 2 * WIDTH), row),
            pl.BlockSpec((1, tm, d), row),
            pl.BlockSpec((1, WIDTH), const2),
            pl.BlockSpec((1, WIDTH), const2),
            pl.BlockSpec(w_out.shape, const2),
            pl.BlockSpec((1, d), const2),
        ],
        out_specs=pl.BlockSpec((1, tm, d), row),
        compiler_params=pltpu.CompilerParams(
            dimension_semantics=("arbitrary", "arbitrary"), vmem_limit_bytes=VMEM_LIMIT),
        name="merge_final" if final else "merge",
    )(o, z, x, ga, gb, w_out, fg)


def _sample_inproj_kernel(x_ref, g_ref, w_ref, wf_ref, bf_ref, u_ref, logf_ref, logft_ref):
    h = _rmsnorm(x_ref[...], g_ref[...]).astype(_bf16)
    u_ref[...] = jnp.dot(h, w_ref[...], preferred_element_type=_f32)
    fa = jnp.dot(h, wf_ref[...], preferred_element_type=_f32) + bf_ref[...]
    logf = _log_sigmoid(fa)
    logf_ref[...] = logf[:, 0:N_FORGET]
    logft_ref[...] = logf.T[0:N_FORGET, :]


def _sample_inproj(x2d, g, w_main, w_f, b_f):
    r, d = x2d.shape
    return pl.pallas_call(
        _sample_inproj_kernel,
        out_shape=(jax.ShapeDtypeStruct((r, w_main.shape[1]), _f32),
                   jax.ShapeDtypeStruct((r, N_FORGET), _f32),
                   jax.ShapeDtypeStruct((N_FORGET, r), _f32)),
        compiler_params=pltpu.CompilerParams(vmem_limit_bytes=VMEM_LIMIT),
        name="sample_inproj",
    )(x2d, g, w_main, w_f, b_f)


def _expand_heads(m, rows_per_head):
    return jnp.concatenate(
        [jnp.broadcast_to(m[h:h + 1, :], (rows_per_head, m.shape[1])) for h in range(N_HEADS)], axis=0)


def _nt_dot(a, b):
    return lax.dot_general(a, b, (((1,), (1,)), ((), ())), preferred_element_type=_f32)


def _head_cols(x, h):
    return x[:, h * HEAD_DIM:(h + 1) * HEAD_DIM]


def _scores_cached(q, kt_ref):
    return jnp.concatenate([jnp.dot(_head_cols(q, h), kt_ref[0, 0, h].astype(_bf16), preferred_element_type=_f32)
                            for h in range(N_HEADS)], axis=0)


def _values_cached(p, vt_ref, tq):
    return jnp.concatenate([_nt_dot(p[h * tq:(h + 1) * tq], vt_ref[0, 0, h].astype(_bf16))
                            for h in range(N_HEADS)], axis=0)


def _scores_new(q, knew):
    return jnp.concatenate([_nt_dot(_head_cols(q, h), _head_cols(knew, h)) for h in range(N_HEADS)], axis=0)


def _values_new(p, vnew, tq):
    return jnp.concatenate([jnp.dot(p[h * tq:(h + 1) * tq], _head_cols(vnew, h), preferred_element_type=_f32)
                            for h in range(N_HEADS)], axis=0)


def _heads_to_lanes(o, tq):
    return jnp.concatenate([o[h * tq:(h + 1) * tq] for h in range(N_HEADS)], axis=1)


def _sample_attn_kernel(u_ref, lfn_ref, lfc_ref, ka_ref, va_ref, kb_ref, vb_ref, tri_ref,
                        oa_ref, ob_ref,
                        cc_ref, xnew_ref, m_ref, l_ref, acca_ref, later_ref, accb_ref, *, tq, nch):
    i = pl.program_id(1)
    kc = SAMPLE_CHUNK
    rows = N_HEADS * tq
    u = u_ref[0]
    qa = (u[:, 0:WIDTH] * SCALE).astype(_bf16)
    qb = (u[:, 4 * WIDTH:5 * WIDTH] * SCALE).astype(_bf16)
    rq = lax.broadcasted_iota(jnp.int32, (rows, LANES), 0) % tq
    ln = lax.broadcasted_iota(jnp.int32, (rows, LANES), 1)
    pad = jnp.zeros((LANES - tq, WIDTH), _f32)

    @pl.when(i == 0)
    def _():
        carry = jnp.zeros((16, 1), _f32)
        zrows = jnp.zeros((16 - N_FORGET, kc), _f32)
        for ch in range(nch):
            x = jnp.concatenate([lfc_ref[0, :, ch * kc:(ch + 1) * kc], zrows], axis=0)
            cch = _lane_cumsum(x, carry)
            cc_ref[ch] = cch
            carry = cch[:, kc - 1:kc]
        xn = jnp.concatenate([lfn_ref[0], jnp.zeros((16 - N_FORGET, LANES), _f32)], axis=0)
        cnew = _lane_cumsum(xn, carry)
        xnew = _expand_heads(cnew[0:N_HEADS, :], tq)
        xnew_ref[...] = xnew
        cq = jnp.sum(jnp.where(ln == rq, xnew, 0.0), axis=1, keepdims=True)

        knew = jnp.concatenate([u[:, WIDTH:2 * WIDTH], pad], axis=0).astype(_bf16)
        vnew = jnp.concatenate([u[:, 2 * WIDTH:3 * WIDTH], pad], axis=0).astype(_bf16)
        logit = jnp.where(ln <= rq, _scores_new(qa, knew) + (cq - xnew), NEG_BIG)
        m = jnp.max(logit, axis=1, keepdims=True)
        p = jnp.exp(logit - m)
        m_ref[...] = m
        l_ref[...] = jnp.sum(p, axis=1, keepdims=True)
        acca_ref[...] = _values_new(p.astype(_bf16), vnew, tq)

        knew = jnp.concatenate([u[:, 5 * WIDTH:6 * WIDTH], pad], axis=0).astype(_bf16)
        vnew = jnp.concatenate([u[:, 6 * WIDTH:7 * WIDTH], pad], axis=0).astype(_bf16)
        z = _scores_new(qb, knew)
        msk = ln < rq
        sp = jnp.where(msk, _softplus(z), 0.0)
        r = lax.broadcasted_iota(jnp.int32, (2 * LANES, LANES), 0)
        c = lax.broadcasted_iota(jnp.int32, (2 * LANES, LANES), 1)
        r = jnp.where(r >= LANES, r - LANES, r)
        suffix = jnp.where(r > c, 1.0, 0.0).astype(_bf16)
        within = jnp.dot(_split2_cols(sp), suffix, preferred_element_type=_f32)
        a = jnp.where(msk, jnp.exp(z - sp - within), 0.0)
        accb_ref[...] = _values_new(a.astype(_bf16), vnew, tq)
        later_ref[...] = jnp.sum(sp, axis=1, keepdims=True)

    xq = xnew_ref[...]
    cq = jnp.sum(jnp.where(ln == rq, xq, 0.0), axis=1, keepdims=True)
    ck = _expand_heads(cc_ref[i][0:N_HEADS, :], tq)
    s = _scores_cached(qa, ka_ref)
    z = _scores_cached(qb, kb_ref)
    logit = s + (cq - ck)
    m_old = m_ref[...]
    m_new = jnp.maximum(m_old, jnp.max(logit, axis=1, keepdims=True))
    alpha = jnp.exp(m_old - m_new)
    sp = _softplus(z)
    within = jnp.dot(_split2_cols(sp), tri_ref[...], preferred_element_type=_f32)
    p = jnp.exp(logit - m_new)
    m_ref[...] = m_new
    l_ref[...] = alpha * l_ref[...] + jnp.sum(p, axis=1, keepdims=True)
    acca_ref[...] = alpha * acca_ref[...] + _values_cached(p.astype(_bf16), va_ref, tq)
    a = jnp.exp(z - sp - within - later_ref[...])
    accb_ref[...] = accb_ref[...] + _values_cached(a.astype(_bf16), vb_ref, tq)
    later_ref[...] = later_ref[...] + jnp.sum(sp, axis=1, keepdims=True)

    @pl.when(i == nch - 1)
    def _():
        oa_ref[0] = _heads_to_lanes(acca_ref[...] / l_ref[...], tq)
        ob_ref[0] = _heads_to_lanes(accb_ref[...], tq)


def _sample_attention(layer, u, lf_new, lf_cache_t, ck_a, cv_a, ck_b, cv_b, tri):
    nb, tq, _ = u.shape
    past = ck_a.shape[4]
    kc = SAMPLE_CHUNK
    nch = past // kc
    rows = N_HEADS * tq
    chunk = (1, 1, N_HEADS, HEAD_DIM, kc)
    fwd = lambda bi, i: (layer, bi, 0, 0, i)
    rev = lambda bi, i: (layer, bi, 0, 0, nch - 1 - i)
    per_b = lambda bi, i: (bi, 0, 0)
    return pl.pallas_call(
        functools.partial(_sample_attn_kernel, tq=tq, nch=nch),
        out_shape=(jax.ShapeDtypeStruct((nb, tq, WIDTH), _f32),
                   jax.ShapeDtypeStruct((nb, tq, WIDTH), _f32)),
        grid=(nb, nch),
        in_specs=[
            pl.BlockSpec((1, tq, u.shape[2]), per_b),
            pl.BlockSpec((1, N_FORGET, LANES), per_b),
            pl.BlockSpec((1, N_FORGET, past), per_b),
            pl.BlockSpec(chunk, fwd),
            pl.BlockSpec(chunk, fwd),
            pl.BlockSpec(chunk, rev),
            pl.BlockSpec(chunk, rev),
            pl.BlockSpec(tri.shape, lambda bi, i: (0, 0)),
        ],
        out_specs=(pl.BlockSpec((1, tq, WIDTH), per_b), pl.BlockSpec((1, tq, WIDTH), per_b)),
        scratch_shapes=[
            pltpu.VMEM((nch, 16, kc), _f32),
            pltpu.VMEM((rows, LANES), _f32),
            pltpu.VMEM((rows, 1), _f32),
            pltpu.VMEM((rows, 1), _f32),
            pltpu.VMEM((rows, HEAD_DIM), _f32),
            pltpu.VMEM((rows, 1), _f32),
            pltpu.VMEM((rows, HEAD_DIM), _f32),
        ],
        compiler_params=pltpu.CompilerParams(
            dimension_semantics=("arbitrary", "arbitrary"), vmem_limit_bytes=VMEM_LIMIT),
        name="sample_attention",
    )(u, lf_new, lf_cache_t, ck_a, cv_a, ck_b, cv_b, tri)


def _split_weights(w):
    f0 = 3 * WIDTH
    w_main = jnp.concatenate([w[:, 0:f0], w[:, f0 + N_FORGET:]], axis=1).astype(_bf16)
    w_f = jnp.pad(w[:, f0:f0 + N_FORGET], ((0, 0), (0, LANES - N_FORGET))).astype(_bf16)
    return w_main, w_f


def kernel(x_prompt, x_sample, cache_fox_k, cache_fox_v, cache_fox_logf, cache_sb_k, cache_sb_v,
           norm_g, w_in, b_f, out_norm_a, out_norm_b, w_out, final_norm_g):
    depth = norm_g.shape[0]
    b, t, d = x_prompt.shape
    nb, tq, _ = x_sample.shape
    past = cache_fox_k.shape[2]
    nkb = t // KEY_BLOCK
    assert t % QUERY_BLOCK == 0 and t % PROJ_ROWS == 0 and past % SAMPLE_CHUNK == 0
    assert tq <= 16 and (nb * tq) % 8 == 0

    kc = SAMPLE_CHUNK
    r = lax.broadcasted_iota(jnp.int32, (2 * kc, kc), 0) % kc
    c = lax.broadcasted_iota(jnp.int32, (2 * kc, kc), 1)
    tri = jnp.where(r > c, 1.0, 0.0).astype(_bf16)
    fg = final_norm_g.reshape(1, d)
    time_minor = lambda a: a.transpose(0, 1, 3, 4, 2)
    cache_fox_k, cache_fox_v = time_minor(cache_fox_k), time_minor(cache_fox_v)
    cache_sb_k, cache_sb_v = time_minor(cache_sb_k), time_minor(cache_sb_v)

    xp, xs = x_prompt, x_sample.reshape(1, nb * tq, d)
    stacked, sample_new = None, []
    for l in range(depth):
        w_main, w_f = _split_weights(w_in[l])
        g = norm_g[l].reshape(1, d)
        bf = jnp.pad(b_f[l], (0, LANES - N_FORGET)).reshape(1, LANES)
        ga, gb = out_norm_a[l].reshape(1, WIDTH), out_norm_b[l].reshape(1, WIDTH)
        wo = w_out[l].astype(_bf16)
        final = l == depth - 1

        outs = _inproj(l, depth, stacked, xp, g, w_main, w_f, bf)
        stacked, (ct, qt, kbf, vt, z) = outs[:N_STACKED], outs[N_STACKED:]
        ct = ct.reshape(b, N_HEADS, nkb, KEY_BLOCK)
        xp = _merge(_attention(qt, kbf, vt, ct), z, xp, ga, gb, wo, fg, final)

        u, logf_s, logft_s = _sample_inproj(xs[0], g, w_main, w_f, bf)
        u3 = u.reshape(nb, tq, u.shape[1])
        lf_new = jnp.pad(logft_s.reshape(N_FORGET, nb, tq).transpose(1, 0, 2),
                         ((0, 0), (0, 0), (0, LANES - tq)))
        lf_cache_t = cache_fox_logf[l].transpose(0, 2, 1)
        oa_s, ob_s = _sample_attention(l, u3, lf_new, lf_cache_t,
                                       cache_fox_k, cache_fox_v, cache_sb_k, cache_sb_v, tri)
        z_s = jnp.concatenate([u[:, 3 * WIDTH:4 * WIDTH], u[:, 7 * WIDTH:8 * WIDTH]], axis=1)
        o_s = jnp.concatenate([oa_s, ob_s], axis=-1)
        xs = _merge(o_s.reshape(1, nb * tq, 2 * WIDTH), z_s.reshape(1, nb * tq, 2 * WIDTH),
                    xs, ga, gb, wo, fg, final)
        hs = lambda a: a.reshape(nb, tq, N_HEADS, HEAD_DIM)
        sample_new.append((hs(u[:, WIDTH:2 * WIDTH]), hs(u[:, 2 * WIDTH:3 * WIDTH]),
                           logf_s.reshape(nb, tq, N_FORGET),
                           hs(u[:, 5 * WIDTH:6 * WIDTH]), hs(u[:, 6 * WIDTH:7 * WIDTH])))

    heads = lambda a: a.reshape(depth, b, N_HEADS, HEAD_DIM, t).transpose(0, 1, 4, 2, 3)
    kat, vat, kbt, vbt, lft = stacked
    stack = lambda items, i: jnp.stack([it[i] for it in items])
    return (xp, xs.reshape(nb, tq, d),
            heads(kat), heads(vat), lft.transpose(0, 1, 3, 2), heads(kbt), heads(vbt),
            *(stack(sample_new, i) for i in range(5)))
```
